```python
import jax, jax.numpy as jnp
from jax import lax
import numpy as np

D_MODEL = 1024
BATCH = 8
SEQ = 4096
DEPTH = 2

CTX_LEN = 256
GRID_W = 64
MIX_WIDTH = D_MODEL
ATTN_WIDTH = MIX_WIDTH // 2
CONV_WIDTH = MIX_WIDTH - ATTN_WIDTH
HEAD_DIM = 64
N_HEADS = ATTN_WIDTH // HEAD_DIM
NA_KH = 8
NA_KW = 16
CONV_K = 3
FF_DIM = 3584
N_EXPERTS = 8
TOP_K = 2
EPS = 1e-6
N_DENSE = (DEPTH + 1) // 2
N_MOE = DEPTH // 2

Q_OFF = 0
K_OFF = ATTN_WIDTH
V_OFF = 2 * ATTN_WIDTH
B_OFF = 3 * ATTN_WIDTH
C_OFF = B_OFF + CONV_WIDTH
U_OFF = C_OFF + CONV_WIDTH
IN_WIDTH = U_OFF + CONV_WIDTH

kernel_name = 'hybrid_na_shortconv_moe_dit'


def rms_norm(x, g):
    xf = x.astype(jnp.float32)
    y = xf * lax.rsqrt(jnp.mean(xf * xf, axis=-1, keepdims=True) + EPS)
    return (y * g.astype(jnp.float32)).astype(x.dtype)


def adaln(c_vec, w, b):
    m = jax.nn.silu(c_vec) @ w + b
    return jnp.split(m, 6, axis=-1)


def heads(t):
    return t.reshape(t.shape[0], t.shape[1], N_HEADS, HEAD_DIM)


def short_gated_conv(p, w):
    bg = p[..., B_OFF:C_OFF]
    cg = p[..., C_OFF:U_OFF]
    u = p[..., U_OFF:IN_WIDTH]
    v = cg * u
    L = v.shape[1]
    pad = CONV_K // 2
    vp = jnp.pad(v, ((0, 0), (pad, CONV_K - 1 - pad), (0, 0)))
    conv = sum(w[i] * vp[:, i:i + L] for i in range(CONV_K))
    return bg * conv


def ctx_attention(q, k, v):
    s = jnp.einsum('bqhd,bkhd->bhqk', q, k).astype(jnp.float32) * (HEAD_DIM ** -0.5)
    p = jax.nn.softmax(s, axis=-1).astype(v.dtype)
    o = jnp.einsum('bhqk,bkhd->bqhd', p, v)
    return o.reshape(o.shape[0], o.shape[1], ATTN_WIDTH)


def neighbourhood_attention(q, k, v, k_c, v_c, rpb):
    b, s = q.shape[0], q.shape[1]
    rows = s // GRID_W
    kh = min(NA_KH, rows)
    scale = HEAD_DIM ** -0.5
    qg = q.reshape(b, rows, GRID_W, N_HEADS, HEAD_DIM)
    kg = k.reshape(b, rows, GRID_W, N_HEADS, HEAD_DIM)
    vg = v.reshape(b, rows, GRID_W, N_HEADS, HEAD_DIM)
    cols = jnp.arange(GRID_W)
    col_start = jnp.clip(cols - NA_KW // 2, 0, GRID_W - NA_KW)
    col_idx = col_start[:, None] + jnp.arange(NA_KW)[None, :]
    dx_idx = col_idx - cols[:, None] + (NA_KW - 1)
    n_loc = kh * NA_KW

    def row_block(r):
        rs = jnp.clip(r - kh // 2, 0, rows - kh)
        q_r = lax.dynamic_index_in_dim(qg, r, axis=1, keepdims=False)
        k_band = lax.dynamic_slice_in_dim(kg, rs, kh, axis=1)
        v_band = lax.dynamic_slice_in_dim(vg, rs, kh, axis=1)
        k_win = k_band[:, :, col_idx]
        v_win = v_band[:, :, col_idx]
        dy_idx = rs + jnp.arange(kh) - r + (NA_KH - 1)
        bias = rpb[:, dy_idx[None, :, None], dx_idx[:, None, :]]
        s_loc = jnp.einsum('bwhd,biwjhd->bhwij', q_r, k_win).astype(jnp.float32) * scale
        s_loc = (s_loc + bias.astype(jnp.float32)[None]).reshape(b, N_HEADS, GRID_W, n_loc)
        s_ctx = jnp.einsum('bwhd,bchd->bhwc', q_r, k_c).astype(jnp.float32) * scale
        p = jax.nn.softmax(jnp.concatenate([s_loc, s_ctx], axis=-1), axis=-1).astype(v.dtype)
        p_loc = p[..., :n_loc].reshape(b, N_HEADS, GRID_W, kh, NA_KW)
        p_ctx = p[..., n_loc:]
        return (jnp.einsum('bhwij,biwjhd->bwhd', p_loc, v_win)
                + jnp.einsum('bhwc,bchd->bwhd', p_ctx, v_c))

    out = lax.map(row_block, jnp.arange(rows))
    return out.transpose(1, 0, 2, 3, 4).reshape(b, s, ATTN_WIDTH)


def merge_groups(attn_out, conv_out, g_out, w_out):
    a = rms_norm(attn_out, g_out[:ATTN_WIDTH])
    cv = rms_norm(conv_out, g_out[ATTN_WIDTH:])
    return jnp.concatenate([a, cv], axis=-1) @ w_out


def swiglu(h, w_gu, w_down):
    g, u = jnp.split(h @ w_gu, 2, axis=-1)
    return (jax.nn.silu(g) * u) @ w_down


def moe_swiglu(h, w_router, w_gu, w_down):
    logits = (h @ w_router).astype(jnp.float32)
    top_v, top_i = lax.top_k(logits, TOP_K)
    top_w = jax.nn.softmax(top_v, axis=-1)
    gates = jnp.sum(jax.nn.one_hot(top_i, N_EXPERTS, dtype=jnp.float32) * top_w[..., None], axis=-2).astype(h.dtype)
    y = jnp.zeros_like(h)
    for e in range(N_EXPERTS):
        y = y + gates[..., e:e + 1] * swiglu(h, w_gu[e], w_down[e])
    return y


def setup_inputs(seed: int = 0) -> dict:
    key = jax.random.key(seed)
    ks = jax.random.split(key, 17)
    f32 = jnp.float32

    def nrm(k, shape, s):
        return jax.random.normal(k, shape, f32) * s

    return {
        'x': nrm(ks[0], (BATCH, SEQ, D_MODEL), 1.0),
        'c': nrm(ks[1], (BATCH, D_MODEL), 1.0),
        'ctx': nrm(ks[2], (BATCH, CTX_LEN, D_MODEL), 1.0),
        'c_ctx': nrm(ks[3], (D_MODEL,), 1.0),
        'w_ada': nrm(ks[4], (DEPTH, D_MODEL, 6 * D_MODEL), 0.5 * D_MODEL ** -0.5),
        'b_ada': nrm(ks[5], (DEPTH, 6 * D_MODEL), 0.02),
        'norm_g': 1.0 + nrm(ks[6], (DEPTH, 4, D_MODEL), 0.05),
        'w_in': nrm(ks[7], (DEPTH, D_MODEL, IN_WIDTH), D_MODEL ** -0.5),
        'rpb': nrm(ks[8], (DEPTH, N_HEADS, 2 * NA_KH - 1, 2 * NA_KW - 1), 0.1),
        'conv_w': nrm(ks[9], (DEPTH, CONV_K, CONV_WIDTH), CONV_K ** -0.5),
        'out_norm_g': 1.0 + nrm(ks[10], (DEPTH, MIX_WIDTH), 0.05),
        'w_out': nrm(ks[11], (DEPTH, MIX_WIDTH, D_MODEL), MIX_WIDTH ** -0.5),
        'w_gu_dense': nrm(ks[12], (N_DENSE, D_MODEL, 2 * FF_DIM), D_MODEL ** -0.5),
        'w_down_dense': nrm(ks[13], (N_DENSE, FF_DIM, D_MODEL), FF_DIM ** -0.5),
        'w_router': nrm(ks[14], (N_MOE, D_MODEL, N_EXPERTS), D_MODEL ** -0.5),
        'w_gu_moe': nrm(ks[15], (N_MOE, N_EXPERTS, D_MODEL, 2 * FF_DIM), D_MODEL ** -0.5),
        'w_down_moe': nrm(ks[16], (N_MOE, N_EXPERTS, FF_DIM, D_MODEL), FF_DIM ** -0.5),
    }


def reference(x, c, ctx, c_ctx, w_ada, b_ada, norm_g, w_in, rpb, conv_w, out_norm_g, w_out,
              w_gu_dense, w_down_dense, w_router, w_gu_moe, w_down_moe):
    def channel_mixer(l, h):
        if l % 2 == 0:
            return swiglu(h, w_gu_dense[l // 2], w_down_dense[l // 2])
        return moe_swiglu(h, w_router[l // 2], w_gu_moe[l // 2], w_down_moe[l // 2])

    for l in range(DEPTH):
        last = l == DEPTH - 1
        sh_m, sc_m, gt_m, sh_f, sc_f, gt_f = adaln(c[:, None, :], w_ada[l], b_ada[l])
        csh_m, csc_m, cgt_m, csh_f, csc_f, cgt_f = adaln(c_ctx, w_ada[l], b_ada[l])
        g_pre_m, g_post_m, g_pre_f, g_post_f = norm_g[l, 0], norm_g[l, 1], norm_g[l, 2], norm_g[l, 3]

        hx = rms_norm(x, g_pre_m) * (1 + sc_m) + sh_m
        hc = rms_norm(ctx, g_pre_m) * (1 + csc_m) + csh_m
        px = hx @ w_in[l]
        if last:
            pc = hc @ w_in[l][:, K_OFF:B_OFF]
            k_c, v_c = heads(pc[..., :ATTN_WIDTH]), heads(pc[..., ATTN_WIDTH:])
        else:
            pc = hc @ w_in[l]
            k_c, v_c = heads(pc[..., K_OFF:V_OFF]), heads(pc[..., V_OFF:B_OFF])

        attn_x = neighbourhood_attention(heads(px[..., Q_OFF:K_OFF]), heads(px[..., K_OFF:V_OFF]),
                                         heads(px[..., V_OFF:B_OFF]), k_c, v_c, rpb[l])
        conv_x = short_gated_conv(px, conv_w[l])
        yx = merge_groups(attn_x, conv_x, out_norm_g[l], w_out[l])
        x = x + gt_m * rms_norm(yx, g_post_m)

        if not last:
            attn_c = ctx_attention(heads(pc[..., Q_OFF:K_OFF]), k_c, v_c)
            conv_c = short_gated_conv(pc, conv_w[l])
            yc = merge_groups(attn_c, conv_c, out_norm_g[l], w_out[l])
            ctx = ctx + cgt_m * rms_norm(yc, g_post_m)

        hx = rms_norm(x, g_pre_f) * (1 + sc_f) + sh_f
        x = x + gt_f * rms_norm(channel_mixer(l, hx), g_post_f)
        if not last:
            hc = rms_norm(ctx, g_pre_f) * (1 + csc_f) + csh_f
            ctx = ctx + cgt_f * rms_norm(channel_mixer(l, hc), g_post_f)
    return x
```

```python
import functools

import numpy as np
import jax
import jax.numpy as jnp
from jax import lax
from jax.experimental import pallas as pl
from jax.experimental.pallas import tpu as pltpu

F32 = jnp.float32
BF16 = jnp.bfloat16
I32 = jnp.int32

EPS = 1e-6
GRID_W = 64
HEAD_DIM = 64
NA_KH = 8
NA_KW = 16
TOP_K = 2
LANES = 128
Q_ROWS = 4
BAND_ROWS = 12
NEG = -1e30
HALO = 16
V7X_VMEM_CAP = 60 * 1024 * 1024
TOKEN_TILE = 512
ROUTE_TILE = 1024
FF_CHUNK = 512


def _cparams(sem, vmem_bytes):
    return pltpu.CompilerParams(dimension_semantics=sem,
                                vmem_limit_bytes=int(min(vmem_bytes, V7X_VMEM_CAP)))


def _rms(x, g):
    return x * lax.rsqrt(jnp.mean(x * x, axis=-1, keepdims=True) + EPS) * g


def _dot(a, b):
    return jnp.dot(a, b, preferred_element_type=F32)


def _dot_nt(a, b):
    return lax.dot_general(a, b, (((1,), (1,)), ((), ())), preferred_element_type=F32)


def _split_bf16(a):
    hi = a.astype(BF16)
    lo = (a - hi.astype(F32)).astype(BF16)
    return hi, lo


def _dot3(a, w):
    ah, al = _split_bf16(a)
    wh, wl = _split_bf16(w)
    return _dot(ah, wh) + (_dot(ah, wl) + _dot(al, wh))


def _silu(x):
    return x / (1.0 + jnp.exp(-x))


def _ada_kernel(c_ref, w_ref, b_ref, o_ref):
    o_ref[0] = _dot3(_silu(c_ref[...]), w_ref[0]) + b_ref[0]


def _ada_call(cc, w_ada, b_ada):
    depth, d, n = w_ada.shape
    rows = cc.shape[0]
    tn = n // 4
    return pl.pallas_call(
        _ada_kernel,
        grid=(depth, n // tn),
        in_specs=[pl.BlockSpec((rows, d), lambda l, j: (0, 0)),
                  pl.BlockSpec((1, d, tn), lambda l, j: (l, 0, j)),
                  pl.BlockSpec((1, 1, tn), lambda l, j: (l, 0, j))],
        out_specs=pl.BlockSpec((1, rows, tn), lambda l, j: (l, 0, j)),
        out_shape=jax.ShapeDtypeStruct((depth, rows, n), F32),
        compiler_params=_cparams(("parallel", "parallel"), 6 * d * tn * 4),
        name="adaln",
    )(cc, w_ada, b_ada.reshape(depth, 1, n))


def _inproj_kernel(x_ref, g_ref, sc_ref, sh_ref, w_ref, *out_refs, full):
    h = _rms(x_ref[0], g_ref[...]) * (1.0 + sc_ref[0]) + sh_ref[0]
    hb = h.astype(BF16)
    cw = out_refs[0].shape[-1]

    def col(j):
        return _dot(hb, w_ref[:, j * cw:(j + 1) * cw])

    if full:
        q_ref, k_ref, v_ref, bg_ref, vcu_ref = out_refs
        q_ref[0] = col(0).astype(BF16)
        k_ref[0] = col(1).astype(BF16)
        v_ref[0] = col(2).astype(BF16)
        bg_ref[0] = col(3).astype(BF16)
        vcu_ref[0] = (col(4) * col(5)).astype(BF16)
    else:
        k_ref, v_ref = out_refs
        k_ref[0] = col(0).astype(BF16)
        v_ref[0] = col(1).astype(BF16)


def _inproj_call(x, g, sc, sh, w, full, name):
    b, s, d = x.shape
    cw = d // 2
    tm = min(TOKEN_TILE, s)
    n_out = 5 if full else 2
    per_batch = sc.shape[0] > 1
    mod_spec = pl.BlockSpec((1, 1, d), (lambda bi, i: (bi, 0, 0)) if per_batch else (lambda bi, i: (0, 0, 0)))
    out_spec = pl.BlockSpec((1, tm, cw), lambda bi, i: (bi, i, 0))
    vmem = 2 * tm * d * 4 + 2 * w.size * 2 + 2 * n_out * tm * cw * 2 + 8 * tm * d * 4
    return pl.pallas_call(
        functools.partial(_inproj_kernel, full=full),
        grid=(b, s // tm),
        in_specs=[pl.BlockSpec((1, tm, d), lambda bi, i: (bi, i, 0)),
                  pl.BlockSpec((1, d), lambda bi, i: (0, 0)),
                  mod_spec, mod_spec,
                  pl.BlockSpec(w.shape, lambda bi, i: (0, 0))],
        out_specs=[out_spec] * n_out,
        out_shape=[jax.ShapeDtypeStruct((b, s, cw), BF16)] * n_out,
        compiler_params=_cparams(("parallel", "parallel"), vmem),
        name=name,
    )(x, g, sc, sh, w)


def _head_pair_attention(q2, score_parts, value_parts, bias_fn):
    lane = lax.broadcasted_iota(I32, q2.shape, 1)
    outs = []
    for hh in range(2):
        in_head = (lane >= hh * HEAD_DIM) & (lane < (hh + 1) * HEAD_DIM)
        qa = jnp.where(in_head, q2 * (HEAD_DIM ** -0.5), 0).astype(BF16)
        scores = []
        for pi, kk in enumerate(score_parts):
            s = _dot_nt(qa, kk)
            bias = bias_fn(hh, pi)
            scores.append(s if bias is None else s + bias)
        m = scores[0].max(axis=-1, keepdims=True)
        for s in scores[1:]:
            m = jnp.maximum(m, s.max(axis=-1, keepdims=True))
        den = None
        acc = None
        for s, vv in zip(scores, value_parts):
            e = jnp.exp(s - m)
            den_i = e.sum(axis=-1, keepdims=True)
            acc_i = _dot(e.astype(BF16), vv)
            den = den_i if den is None else den + den_i
            acc = acc_i if acc is None else acc + acc_i
        outs.append(acc / den)
    lane_o = lax.broadcasted_iota(I32, outs[0].shape, 1)
    return jnp.where(lane_o < HEAD_DIM, outs[0], outs[1])


def _na_kernel(q_ref, k_ref, v_ref, kc_ref, vc_ref, bias_ref, o_ref, *, grid_rows):
    i = pl.program_id(1)
    start_row = jnp.clip(Q_ROWS * i - NA_KH // 2, 0, grid_rows - BAND_ROWS)
    tok0 = pl.multiple_of(start_row * GRID_W, GRID_W)
    band = BAND_ROWS * GRID_W
    n_pairs = q_ref.shape[-1] // LANES
    for p in range(n_pairs):
        ls = slice(p * LANES, (p + 1) * LANES)
        k2 = k_ref[0, pl.ds(tok0, band), ls]
        v2 = v_ref[0, pl.ds(tok0, band), ls]
        o2 = _head_pair_attention(
            q_ref[0, :, ls],
            [k2, kc_ref[0, :, ls]], [v2, vc_ref[0, :, ls]],
            lambda hh, pi, p=p: bias_ref[0, 2 * p + hh] if pi == 0 else None)
        o_ref[0, :, ls] = o2.astype(BF16)


def _na_call(q, k, v, kc, vc, bias):
    b, s, cw = q.shape
    ctx = kc.shape[1]
    grid_rows = s // GRID_W
    tq = Q_ROWS * GRID_W
    nblk = s // tq
    band = BAND_ROWS * GRID_W
    n_heads = cw // HEAD_DIM

    def bias_map(bi, i):
        return (jnp.where(i == 0, 0, jnp.where(i == nblk - 1, 2, 1)), 0, 0, 0)

    vmem = (4 * s * cw * 2 + 4 * ctx * cw * 2 + 2 * n_heads * tq * band * 4
            + 4 * tq * cw * 2 + 12 * tq * (band + ctx) * 4)
    return pl.pallas_call(
        functools.partial(_na_kernel, grid_rows=grid_rows),
        grid=(b, nblk),
        in_specs=[pl.BlockSpec((1, tq, cw), lambda bi, i: (bi, i, 0)),
                  pl.BlockSpec((1, s, cw), lambda bi, i: (bi, 0, 0)),
                  pl.BlockSpec((1, s, cw), lambda bi, i: (bi, 0, 0)),
                  pl.BlockSpec((1, ctx, cw), lambda bi, i: (bi, 0, 0)),
                  pl.BlockSpec((1, ctx, cw), lambda bi, i: (bi, 0, 0)),
                  pl.BlockSpec((1, n_heads, tq, band), bias_map)],
        out_specs=pl.BlockSpec((1, tq, cw), lambda bi, i: (bi, i, 0)),
        out_shape=jax.ShapeDtypeStruct((b, s, cw), BF16),
        compiler_params=_cparams(("parallel", "arbitrary"), vmem),
        name="neigh_attn",
    )(q, k, v, kc, vc, bias)


def _ctx_attn_kernel(q_ref, k_ref, v_ref, o_ref):
    n_pairs = q_ref.shape[-1] // LANES
    for p in range(n_pairs):
        ls = slice(p * LANES, (p + 1) * LANES)
        o2 = _head_pair_attention(q_ref[0, :, ls], [k_ref[0, :, ls]], [v_ref[0, :, ls]],
                                  lambda hh, pi: None)
        o_ref[0, :, ls] = o2.astype(BF16)


def _ctx_attn_call(q, k, v):
    b, s, cw = q.shape
    spec = pl.BlockSpec((1, s, cw), lambda bi: (bi, 0, 0))
    return pl.pallas_call(
        _ctx_attn_kernel,
        grid=(b,),
        in_specs=[spec, spec, spec],
        out_specs=spec,
        out_shape=jax.ShapeDtypeStruct((b, s, cw), BF16),
        compiler_params=_cparams(("parallel",), 8 * s * cw * 2 + 16 * s * s * 4),
        name="ctx_attn",
    )(q, k, v)


def _bias_tables(rpb_l, grid_rows):
    w = GRID_W
    j = np.arange(Q_ROWS)[:, None, None, None]
    qc = np.arange(w)[None, :, None, None]
    a = np.arange(BAND_ROWS)[None, None, :, None]
    kc = np.arange(w)[None, None, None, :]
    idx, valid = [], []
    for r0, start in ((0, 0), (Q_ROWS, 0), (grid_rows - Q_ROWS, grid_rows - BAND_ROWS)):
        r = r0 + j
        rs = np.clip(r - NA_KH // 2, 0, grid_rows - NA_KH)
        kr = start + a
        cs = np.clip(qc - NA_KW // 2, 0, w - NA_KW)
        ok = (kr >= rs) & (kr < rs + NA_KH) & (kc >= cs) & (kc < cs + NA_KW)
        dy = np.clip(kr - r + NA_KH - 1, 0, 2 * NA_KH - 2)
        dx = np.clip(kc - qc + NA_KW - 1, 0, 2 * NA_KW - 2)
        shape = (Q_ROWS, w, BAND_ROWS, w)
        flat = (Q_ROWS * w, BAND_ROWS * w)
        idx.append(np.broadcast_to(dy * (2 * NA_KW - 1) + dx, shape).reshape(flat))
        valid.append(np.broadcast_to(ok, shape).reshape(flat))
    idx = jnp.asarray(np.stack(idx), dtype=I32)
    valid = jnp.asarray(np.stack(valid))
    n_heads = rpb_l.shape[0]
    table = jnp.take(rpb_l.reshape(n_heads, -1), idx, axis=1)
    table = jnp.where(valid[None], table, NEG)
    return table.transpose(1, 0, 2, 3)


def _merge_kernel(attn_ref, vcu_ref, vprev_ref, vnext_ref, bg_ref, x_ref, wout_ref, convw_ref,
                  gout_ref, gpost_ref, gt_ref, gpre_ref, sc_ref, sh_ref, *rest, n_exp):
    moe = n_exp > 0
    i = pl.program_id(1)
    n = pl.num_programs(1)
    v = vcu_ref[0].astype(F32)
    tm, cw = v.shape
    row = lax.broadcasted_iota(I32, v.shape, 0)
    prev = jnp.where(i > 0, vprev_ref[0, HALO - 1:HALO, :].astype(F32), 0.0)
    nxt = jnp.where(i < n - 1, vnext_ref[0, 0:1, :].astype(F32), 0.0)
    v_m1 = jnp.where(row == 0, prev, pltpu.roll(v, 1, 0))
    v_p1 = jnp.where(row == tm - 1, nxt, pltpu.roll(v, tm - 1, 0))
    w3 = convw_ref[...]
    conv = w3[0:1] * v_m1 + w3[1:2] * v + w3[2:3] * v_p1
    cv = bg_ref[0].astype(F32) * conv
    g_out = gout_ref[...]
    a = _rms(attn_ref[0].astype(F32), g_out[:, :cw])
    c = _rms(cv, g_out[:, cw:])
    y = _dot(a.astype(BF16), wout_ref[:cw, :]) + _dot(c.astype(BF16), wout_ref[cw:, :])
    xn = x_ref[0] + gt_ref[0] * _rms(y, gpost_ref[...])
    h = _rms(xn, gpre_ref[...]) * (1.0 + sc_ref[0]) + sh_ref[0]
    if not moe:
        x_out_ref, h_ref = rest
        x_out_ref[0] = xn
        h_ref[0] = h.astype(BF16)
        return
    wr_ref, x_out_ref, h_ref, ti_ref, tw_ref = rest
    x_out_ref[0] = xn
    h_ref[0] = h
    lane = lax.broadcasted_iota(I32, (tm, LANES), 1).astype(F32)
    logits = jnp.where(lane < n_exp, _dot3(h, wr_ref[...]), NEG)
    big = float(LANES)
    m1 = logits.max(axis=-1, keepdims=True)
    i1 = jnp.where(logits == m1, lane, big).min(axis=-1, keepdims=True)
    l2 = jnp.where(lane == i1, NEG, logits)
    m2 = l2.max(axis=-1, keepdims=True)
    i2 = jnp.where(l2 == m2, lane, big).min(axis=-1, keepdims=True)
    e = jnp.exp(m2 - m1)
    w1 = 1.0 / (1.0 + e)
    ti_ref[0] = jnp.concatenate([i1, i2], axis=1).astype(I32)
    tw_ref[0] = jnp.concatenate([w1, e * w1], axis=1)


def _merge_call(attn, vcu, bg, x, wout, convw, gout, gpost, gt, gpre, sc, sh, w_router, name):
    b, s, d = x.shape
    cw = attn.shape[-1]
    tm = min(TOKEN_TILE, s)
    nt = s // tm
    moe = w_router is not None
    n_exp = w_router.shape[-1] if moe else 0
    if moe:
        wr = jnp.zeros((d, LANES), F32).at[:, :n_exp].set(w_router)
    per_batch = gt.shape[0] > 1
    hb = tm // HALO
    n_halo = s // HALO
    tile = lambda width: pl.BlockSpec((1, tm, width), lambda bi, i: (bi, i, 0))
    vec = lambda width: pl.BlockSpec((1, width), lambda bi, i: (0, 0))
    mod = pl.BlockSpec((1, 1, d), (lambda bi, i: (bi, 0, 0)) if per_batch else (lambda bi, i: (0, 0, 0)))
    in_specs = [tile(cw), tile(cw),
                pl.BlockSpec((1, HALO, cw), lambda bi, i: (bi, jnp.maximum(i * hb - 1, 0), 0)),
                pl.BlockSpec((1, HALO, cw), lambda bi, i: (bi, jnp.minimum((i + 1) * hb, n_halo - 1), 0)),
                tile(cw), tile(d),
                pl.BlockSpec(wout.shape, lambda bi, i: (0, 0)),
                pl.BlockSpec(convw.shape, lambda bi, i: (0, 0)),
                vec(d), vec(d), mod, vec(d), mod, mod]
    args = [attn, vcu, vcu, vcu, bg, x, wout, convw, gout, gpost, gt, gpre, sc, sh]
    out_specs = [tile(d), tile(d)]
    out_shape = [jax.ShapeDtypeStruct((b, s, d), F32),
                 jax.ShapeDtypeStruct((b, s, d), F32 if moe else BF16)]
    if moe:
        in_specs.append(pl.BlockSpec(wr.shape, lambda bi, i: (0, 0)))
        args.append(wr)
        out_specs += [tile(TOP_K), tile(TOP_K)]
        out_shape += [jax.ShapeDtypeStruct((b, s, TOP_K), I32), jax.ShapeDtypeStruct((b, s, TOP_K), F32)]
    vmem = 2 * wout.size * 2 + 2 * tm * (3 * cw * 2 + 3 * d * 4) + 16 * tm * d * 4
    return pl.pallas_call(
        functools.partial(_merge_kernel, n_exp=n_exp),
        grid=(b, nt),
        in_specs=in_specs,
        out_specs=out_specs,
        out_shape=out_shape,
        compiler_params=_cparams(("parallel", "arbitrary"), vmem),
        name=name,
    )(*args)


def _swiglu(hb, wgu, wdn, ff):
    chunk = min(FF_CHUNK, ff)
    acc = None
    for c in range(ff // chunk):
        lo, hi = c * chunk, (c + 1) * chunk
        g = _dot(hb, wgu(lo, hi))
        u = _dot(hb, wgu(ff + lo, ff + hi))
        part = _dot((_silu(g) * u).astype(BF16), wdn(lo, hi))
        acc = part if acc is None else acc + part
    return acc


def _ffn_kernel(h_ref, x_ref, wgu_ref, wdn_ref, gpost_ref, gt_ref, o_ref):
    ff = wdn_ref.shape[0]
    y = _swiglu(h_ref[0], lambda lo, hi: wgu_ref[:, lo:hi], lambda lo, hi: wdn_ref[lo:hi, :], ff)
    o_ref[0] = x_ref[0] + gt_ref[0] * _rms(y, gpost_ref[...])


def _ffn_call(h, x, wgu, wdn, gpost, gt, name):
    b, s, d = x.shape
    tm = min(TOKEN_TILE, s)
    per_batch = gt.shape[0] > 1
    tile = pl.BlockSpec((1, tm, d), lambda bi, i: (bi, i, 0))
    vmem = (wgu.size + wdn.size) * 2 + 2 * tm * d * (2 + 4 + 4) + 10 * tm * d * 4
    return pl.pallas_call(
        _ffn_kernel,
        grid=(b, s // tm),
        in_specs=[tile, tile,
                  pl.BlockSpec(wgu.shape, lambda bi, i: (0, 0)),
                  pl.BlockSpec(wdn.shape, lambda bi, i: (0, 0)),
                  pl.BlockSpec((1, d), lambda bi, i: (0, 0)),
                  pl.BlockSpec((1, 1, d), (lambda bi, i: (bi, 0, 0)) if per_batch else (lambda bi, i: (0, 0, 0)))],
        out_specs=tile,
        out_shape=jax.ShapeDtypeStruct((b, s, d), F32),
        compiler_params=_cparams(("parallel", "parallel"), vmem),
        name=name,
    )(h, x, wgu, wdn, gpost, gt)


def _plan_kernel(ti_ref, rank_ref, cnt_ref, carry):
    @pl.when(pl.program_id(0) == 0)
    def _():
        carry[...] = jnp.zeros_like(carry)

    ti = ti_ref[...]
    tm = ti.shape[0]
    lane = lax.broadcasted_iota(I32, (tm, LANES), 1)
    e1 = lane == ti[:, 0:1]
    e2 = lane == ti[:, 1:2]
    c = e1.astype(F32) + e2.astype(F32)
    tri = (lax.broadcasted_iota(I32, (tm, tm), 1) < lax.broadcasted_iota(I32, (tm, tm), 0)).astype(BF16)
    before = _dot(tri, c.astype(BF16)) + carry[...]
    r1 = jnp.where(e1, before, 0.0).sum(axis=-1, keepdims=True)
    r2 = jnp.where(e2, before, 0.0).sum(axis=-1, keepdims=True)
    rank_ref[...] = jnp.concatenate([r1, r2], axis=1).astype(I32)
    carry[...] += c.sum(axis=0, keepdims=True)
    cnt_ref[...] = carry[...]


def _plan_call(ti):
    n = ti.shape[0]
    tm = min(TOKEN_TILE, n)
    return pl.pallas_call(
        _plan_kernel,
        grid=(n // tm,),
        in_specs=[pl.BlockSpec((tm, TOP_K), lambda j: (j, 0))],
        out_specs=[pl.BlockSpec((tm, TOP_K), lambda j: (j, 0)),
                   pl.BlockSpec((1, LANES), lambda j: (0, 0))],
        out_shape=[jax.ShapeDtypeStruct((n, TOP_K), I32), jax.ShapeDtypeStruct((1, LANES), F32)],
        scratch_shapes=[pltpu.VMEM((1, LANES), F32)],
        compiler_params=_cparams(("arbitrary",), 8 * tm * tm * 4 + 16 * tm * LANES * 4),
        name="route_plan",
    )(ti)


def _row_copy(src, src_row, dst, dst_row, sem):
    return pltpu.make_async_copy(src.at[pl.ds(src_row, 1), :], dst.at[pl.ds(dst_row, 1), :], sem)


def _dispatch_kernel(p1_ref, p2_ref, h_ref, init_ref, o_ref, sem):
    del init_ref
    tm = h_ref.shape[0]
    base = pl.program_id(0) * tm

    def issue(t, carry):
        _row_copy(h_ref, t, o_ref, p1_ref[base + t], sem).start()
        _row_copy(h_ref, t, o_ref, p2_ref[base + t], sem).start()
        return carry

    def drain(t, carry):
        _row_copy(h_ref, 0, o_ref, 0, sem).wait()
        _row_copy(h_ref, 0, o_ref, 0, sem).wait()
        return carry

    lax.fori_loop(0, tm, issue, 0)
    lax.fori_loop(0, tm, drain, 0)


def _dispatch_call(pos1, pos2, h, n_rows):
    n, d = h.shape
    tm = min(ROUTE_TILE, n)
    init = jnp.zeros((n_rows, d), F32)
    return pl.pallas_call(
        _dispatch_kernel,
        grid_spec=pltpu.PrefetchScalarGridSpec(
            num_scalar_prefetch=2,
            grid=(n // tm,),
            in_specs=[pl.BlockSpec((tm, d), lambda j, p1, p2: (j, 0)),
                      pl.BlockSpec(memory_space=pl.ANY)],
            out_specs=pl.BlockSpec(memory_space=pl.ANY),
            scratch_shapes=[pltpu.SemaphoreType.DMA(())]),
        out_shape=jax.ShapeDtypeStruct((n_rows, d), F32),
        input_output_aliases={3: 0},
        compiler_params=_cparams(("arbitrary",), 4 * tm * d * 4),
        name="route_dispatch",
    )(pos1, pos2, h, init)


def _experts_kernel(te_ref, tv_ref, h_ref, wgu_ref, wdn_ref, y_ref):
    j = pl.program_id(0)
    ff = wdn_ref.shape[1]

    @pl.when(tv_ref[j] > 0)
    def _():
        y_ref[...] = _swiglu(h_ref[...].astype(BF16),
                             lambda lo, hi: wgu_ref[0, :, lo:hi],
                             lambda lo, hi: wdn_ref[0, lo:hi, :], ff)

    @pl.when(tv_ref[j] == 0)
    def _():
        y_ref[...] = jnp.zeros_like(y_ref)


def _experts_call(tile_expert, tile_valid, h_sorted, wgu, wdn, tm):
    n_rows, d = h_sorted.shape
    n_exp, _, ff2 = wgu.shape
    ff = wdn.shape[1]
    vmem = (wgu.size + wdn.size) // n_exp * 2 + 4 * tm * d * 4 + 10 * tm * d * 4
    return pl.pallas_call(
        _experts_kernel,
        grid_spec=pltpu.PrefetchScalarGridSpec(
            num_scalar_prefetch=2,
            grid=(n_rows // tm,),
            in_specs=[pl.BlockSpec((tm, d), lambda j, te, tv: (j, 0)),
                      pl.BlockSpec((1, d, ff2), lambda j, te, tv: (te[j], 0, 0),
                                   pipeline_mode=pl.Buffered(1)),
                      pl.BlockSpec((1, ff, d), lambda j, te, tv: (te[j], 0, 0),
                                   pipeline_mode=pl.Buffered(1))],
            out_specs=pl.BlockSpec((tm, d), lambda j, te, tv: (j, 0))),
        out_shape=jax.ShapeDtypeStruct((n_rows, d), F32),
        compiler_params=_cparams(("arbitrary",), vmem),
        name="route_experts",
    )(tile_expert, tile_valid, h_sorted, wgu, wdn)


def _combine_kernel(p1_ref, p2_ref, tw_ref, x_ref, gt_ref, gpost_ref, y_ref, o_ref, buf, sem):
    tm = x_ref.shape[0]
    base = pl.program_id(0) * tm

    def issue(t, carry):
        _row_copy(y_ref, p1_ref[base + t], buf.at[0], t, sem).start()
        _row_copy(y_ref, p2_ref[base + t], buf.at[1], t, sem).start()
        return carry

    def drain(t, carry):
        _row_copy(y_ref, 0, buf.at[0], 0, sem).wait()
        _row_copy(y_ref, 0, buf.at[1], 0, sem).wait()
        return carry

    lax.fori_loop(0, tm, issue, 0)
    lax.fori_loop(0, tm, drain, 0)
    tw = tw_ref[...]
    y = tw[:, 0:1] * buf[0] + tw[:, 1:2] * buf[1]
    o_ref[...] = x_ref[...] + gt_ref[0] * _rms(y, gpost_ref[...])


def _combine_call(pos1, pos2, tw, x, gt, gpost, y_sorted, seq):
    n, d = x.shape
    tm = min(ROUTE_TILE, seq)
    per_seq = seq // tm
    return pl.pallas_call(
        _combine_kernel,
        grid_spec=pltpu.PrefetchScalarGridSpec(
            num_scalar_prefetch=2,
            grid=(n // tm,),
            in_specs=[pl.BlockSpec((tm, TOP_K), lambda j, p1, p2: (j, 0)),
                      pl.BlockSpec((tm, d), lambda j, p1, p2: (j, 0)),
                      pl.BlockSpec((1, 1, d), lambda j, p1, p2: (j // per_seq, 0, 0)),
                      pl.BlockSpec((1, d), lambda j, p1, p2: (0, 0)),
                      pl.BlockSpec(memory_space=pl.ANY)],
            out_specs=pl.BlockSpec((tm, d), lambda j, p1, p2: (j, 0)),
            scratch_shapes=[pltpu.VMEM((TOP_K, tm, d), F32), pltpu.SemaphoreType.DMA(())]),
        out_shape=jax.ShapeDtypeStruct((n, d), F32),
        compiler_params=_cparams(("arbitrary",), 12 * tm * d * 4),
        name="route_combine",
    )(pos1, pos2, tw, x, gt, gpost, y_sorted)


def _moe_call(h, ti, tw, x, gt, gpost, wgu, wdn):
    b, s, d = x.shape
    n = b * s
    n_exp = wgu.shape[0]
    tm = TOKEN_TILE
    ti = ti.reshape(n, TOP_K)
    rank, cnt = _plan_call(ti)
    counts = cnt[0, :n_exp].astype(I32)
    padded = (counts + tm - 1) // tm * tm
    ends = jnp.cumsum(padded)
    starts = ends - padded
    experts = jnp.arange(n_exp, dtype=I32)
    pos = rank + jnp.sum(jnp.where(ti[..., None] == experts, starts, 0), axis=-1)
    n_tiles = n * TOP_K // tm + n_exp
    row0 = jnp.arange(n_tiles, dtype=I32) * tm
    tile_expert = jnp.minimum(jnp.sum(row0[:, None] >= ends[None, :], axis=1), n_exp - 1).astype(I32)
    tile_valid = (row0 < ends[-1]).astype(I32)
    pos1, pos2 = pos[:, 0], pos[:, 1]
    h_sorted = _dispatch_call(pos1, pos2, h.reshape(n, d), n_tiles * tm)
    y_sorted = _experts_call(tile_expert, tile_valid, h_sorted, wgu, wdn, tm)
    out = _combine_call(pos1, pos2, tw.reshape(n, TOP_K), x.reshape(n, d), gt, gpost, y_sorted, s)
    return out.reshape(b, s, d)


def kernel(x, c, ctx, c_ctx, w_ada, b_ada, norm_g, w_in, rpb, conv_w, out_norm_g, w_out,
           w_gu_dense, w_down_dense, w_router, w_gu_moe, w_down_moe):
    depth = w_ada.shape[0]
    b, s, d = x.shape
    cw = d // 2
    n_exp = w_router.shape[-1]

    cc = jnp.zeros((16, d), F32).at[:b].set(c).at[b].set(c_ctx)
    mods = _ada_call(cc, w_ada, b_ada)

    for l in range(depth):
        last = l == depth - 1
        m = mods[l].reshape(16, 6, d)
        lat = [m[:b, j][:, None, :] for j in range(6)]
        cxm = [m[b:b + 1, j][:, None, :] for j in range(6)]
        g = [norm_g[l, j][None, :] for j in range(4)]
        w_in_l = w_in[l].astype(BF16)
        w_out_l = w_out[l].astype(BF16)
        gout = out_norm_g[l][None, :]

        q, k, v, bg, vcu = _inproj_call(x, g[0], lat[1], lat[0], w_in_l, True, f"inproj_x{l}")
        if last:
            kc, vc = _inproj_call(ctx, g[0], cxm[1], cxm[0], w_in_l[:, cw:3 * cw], False, f"inproj_c{l}")
        else:
            qc, kc, vc, bgc, vcuc = _inproj_call(ctx, g[0], cxm[1], cxm[0], w_in_l, True, f"inproj_c{l}")

        attn = _na_call(q, k, v, kc, vc, _bias_tables(rpb[l], s // GRID_W))
        moe = l % 2 == 1
        x, h, *route = _merge_call(attn, vcu, bg, x, w_out_l, conv_w[l], gout, g[1], lat[2], g[2],
                                   lat[4], lat[3], w_router[l // 2] if moe else None, f"merge_x{l}")
        if not last:
            attn_c = _ctx_attn_call(qc, kc, vc)
            ctx, hc = _merge_call(attn_c, vcuc, bgc, ctx, w_out_l, conv_w[l], gout, g[1], cxm[2], g[2],
                                  cxm[4], cxm[3], None, f"merge_c{l}")

        if moe:
            x = _moe_call(h, route[0], route[1], x, lat[5], g[3],
                          w_gu_moe[l // 2].astype(BF16), w_down_moe[l // 2].astype(BF16))
        else:
            wgu = w_gu_dense[l // 2].astype(BF16)
            wdn = w_down_dense[l // 2].astype(BF16)
            x = _ffn_call(h, x, wgu, wdn, g[3], lat[5], f"ffn_x{l}")
            if not last:
                ctx = _ffn_call(hc, ctx, wgu, wdn, g[3], cxm[5], f"ffn_c{l}")
    return x
```

```python
import functools

import numpy as np
import jax
import jax.numpy as jnp
from jax import lax
from jax.experimental import pallas as pl
from jax.experimental.pallas import tpu as pltpu

F32 = jnp.float32
BF16 = jnp.bfloat16
I32 = jnp.int32

EPS = 1e-6
GRID_W = 64
HEAD_DIM = 64
NA_KH = 8
NA_KW = 16
TOP_K = 2
LANES = 128
Q_ROWS = 4
BAND_ROWS = 12
ATTN_BLOCKS_PER_STEP = 4
NEG = -1e30
HALO = 16
V7X_VMEM_CAP = 60 * 1024 * 1024
TOKEN_TILE = 512
ROUTE_TILE = 1024
FF_CHUNK = 512
ROW_DMA_UNROLL = 8
EXPERT_ROWS = 16


def _cparams(sem, vmem_bytes):
    return pltpu.CompilerParams(dimension_semantics=sem,
                                vmem_limit_bytes=int(min(vmem_bytes, V7X_VMEM_CAP)))


def _rms(x, g):
    return x * lax.rsqrt(jnp.mean(x * x, axis=-1, keepdims=True) + EPS) * g


def _dot(a, b):
    return jnp.dot(a, b, preferred_element_type=F32)


def _dot_nt(a, b):
    return lax.dot_general(a, b, (((1,), (1,)), ((), ())), preferred_element_type=F32)


def _split_bf16(a):
    hi = a.astype(BF16)
    lo = (a - hi.astype(F32)).astype(BF16)
    return hi, lo


def _dot3(a, w):
    ah, al = _split_bf16(a)
    wh, wl = _split_bf16(w)
    return _dot(ah, wh) + (_dot(ah, wl) + _dot(al, wh))


def _silu(x):
    return x / (1.0 + jnp.exp(-x))


def _ada_kernel(c_ref, w_ref, b_ref, o_ref):
    o_ref[0] = _dot3(_silu(c_ref[...]), w_ref[0]) + b_ref[0]


def _ada_call(cc, w_ada, b_ada):
    depth, d, n = w_ada.shape
    rows = cc.shape[0]
    tn = n // 4
    return pl.pallas_call(
        _ada_kernel,
        grid=(depth, n // tn),
        in_specs=[pl.BlockSpec((rows, d), lambda l, j: (0, 0)),
                  pl.BlockSpec((1, d, tn), lambda l, j: (l, 0, j)),
                  pl.BlockSpec((1, 1, tn), lambda l, j: (l, 0, j))],
        out_specs=pl.BlockSpec((1, rows, tn), lambda l, j: (l, 0, j)),
        out_shape=jax.ShapeDtypeStruct((depth, rows, n), F32),
        compiler_params=_cparams(("parallel", "parallel"), 6 * d * tn * 4),
        name="adaln",
    )(cc, w_ada, b_ada.reshape(depth, 1, n))


def _inproj_kernel(x_ref, g_ref, sc_ref, sh_ref, w_ref, *out_refs, full):
    h = _rms(x_ref[0], g_ref[...]) * (1.0 + sc_ref[0]) + sh_ref[0]
    hb = h.astype(BF16)
    cw = out_refs[0].shape[-1]

    def col(j):
        return _dot(hb, w_ref[:, j * cw:(j + 1) * cw])

    if full:
        q_ref, k_ref, v_ref, bg_ref, vcu_ref = out_refs
        q_ref[0] = col(0).astype(BF16)
        k_ref[0] = col(1).astype(BF16)
        v_ref[0] = col(2).astype(BF16)
        bg_ref[0] = col(3).astype(BF16)
        vcu_ref[0] = (col(4) * col(5)).astype(BF16)
    else:
        k_ref, v_ref = out_refs
        k_ref[0] = col(0).astype(BF16)
        v_ref[0] = col(1).astype(BF16)


def _inproj_call(x, g, sc, sh, w, full, name):
    b, s, d = x.shape
    cw = d // 2
    tm = min(TOKEN_TILE, s)
    n_out = 5 if full else 2
    per_batch = sc.shape[0] > 1
    mod_spec = pl.BlockSpec((1, 1, d), (lambda bi, i: (bi, 0, 0)) if per_batch else (lambda bi, i: (0, 0, 0)))
    out_spec = pl.BlockSpec((1, tm, cw), lambda bi, i: (bi, i, 0))
    vmem = 2 * tm * d * 4 + 2 * w.size * 2 + 2 * n_out * tm * cw * 2 + 8 * tm * d * 4
    return pl.pallas_call(
        functools.partial(_inproj_kernel, full=full),
        grid=(b, s // tm),
        in_specs=[pl.BlockSpec((1, tm, d), lambda bi, i: (bi, i, 0)),
                  pl.BlockSpec((1, d), lambda bi, i: (0, 0)),
                  mod_spec, mod_spec,
                  pl.BlockSpec(w.shape, lambda bi, i: (0, 0))],
        out_specs=[out_spec] * n_out,
        out_shape=[jax.ShapeDtypeStruct((b, s, cw), BF16)] * n_out,
        compiler_params=_cparams(("parallel", "parallel"), vmem),
        name=name,
    )(x, g, sc, sh, w)


def _head_pair_attention(q2, key_parts, value_parts, bias_parts):
    m_rows = q2.shape[0]
    lane = lax.broadcasted_iota(I32, q2.shape, 1)
    qs = q2 * (HEAD_DIM ** -0.5)
    qq = jnp.concatenate([jnp.where(lane < HEAD_DIM, qs, 0), jnp.where(lane >= HEAD_DIM, qs, 0)], axis=0)
    scores = []
    for kk, bias in zip(key_parts, bias_parts):
        s = _dot_nt(qq, kk)
        scores.append(s if bias is None else s + bias)
    m = scores[0].max(axis=-1, keepdims=True)
    for s in scores[1:]:
        m = jnp.maximum(m, s.max(axis=-1, keepdims=True))
    acc = None
    for s, vv in zip(scores, value_parts):
        v_aug = jnp.concatenate([vv, jnp.ones_like(vv)], axis=1)
        acc_i = _dot(jnp.exp(s - m).astype(BF16), v_aug)
        acc = acc_i if acc is None else acc + acc_i
    out = acc[:, :LANES] / acc[:, LANES:]
    return jnp.where(lane < HEAD_DIM, out[:m_rows], out[m_rows:])


def _na_kernel(q_ref, k_ref, v_ref, kc_ref, vc_ref, bias_ref, o_ref, *, grid_rows, blocks_per_step):
    tq = Q_ROWS * GRID_W
    band = BAND_ROWS * GRID_W
    n_blocks = grid_rows // Q_ROWS
    n_pairs = q_ref.shape[-1] // LANES
    for jb in range(blocks_per_step):
        ib = pl.program_id(1) * blocks_per_step + jb
        start_row = jnp.clip(Q_ROWS * ib - NA_KH // 2, 0, grid_rows - BAND_ROWS)
        tok0 = pl.multiple_of(start_row * GRID_W, GRID_W)
        kind = jnp.where(ib == 0, 0, jnp.where(ib == n_blocks - 1, 2, 1))
        rows = slice(jb * tq, (jb + 1) * tq)
        for p in range(n_pairs):
            ls = slice(p * LANES, (p + 1) * LANES)
            k2 = k_ref[0, pl.ds(tok0, band), ls]
            v2 = v_ref[0, pl.ds(tok0, band), ls]
            o2 = _head_pair_attention(q_ref[0, rows, ls], [k2, kc_ref[0, :, ls]], [v2, vc_ref[0, :, ls]],
                                      [bias_ref[kind, p], None])
            o_ref[0, rows, ls] = o2.astype(BF16)


def _na_call(q, k, v, kc, vc, bias):
    b, s, cw = q.shape
    ctx = kc.shape[1]
    grid_rows = s // GRID_W
    tq = Q_ROWS * GRID_W
    nblk = s // tq
    bps = min(ATTN_BLOCKS_PER_STEP, nblk)
    band = BAND_ROWS * GRID_W
    n_heads = cw // HEAD_DIM
    bias = bias.reshape(3, n_heads // 2, 2 * tq, band)
    vmem = (4 * s * cw * 2 + 4 * ctx * cw * 2 + bias.size * 4
            + 4 * bps * tq * cw * 2 + 16 * tq * (band + ctx) * 4)
    return pl.pallas_call(
        functools.partial(_na_kernel, grid_rows=grid_rows, blocks_per_step=bps),
        grid=(b, nblk // bps),
        in_specs=[pl.BlockSpec((1, bps * tq, cw), lambda bi, i: (bi, i, 0)),
                  pl.BlockSpec((1, s, cw), lambda bi, i: (bi, 0, 0)),
                  pl.BlockSpec((1, s, cw), lambda bi, i: (bi, 0, 0)),
                  pl.BlockSpec((1, ctx, cw), lambda bi, i: (bi, 0, 0)),
                  pl.BlockSpec((1, ctx, cw), lambda bi, i: (bi, 0, 0)),
                  pl.BlockSpec(bias.shape, lambda bi, i: (0, 0, 0, 0))],
        out_specs=pl.BlockSpec((1, bps * tq, cw), lambda bi, i: (bi, i, 0)),
        out_shape=jax.ShapeDtypeStruct((b, s, cw), BF16),
        compiler_params=_cparams(("parallel", "arbitrary"), vmem),
        name="neigh_attn",
    )(q, k, v, kc, vc, bias)


def _ctx_attn_kernel(q_ref, k_ref, v_ref, o_ref):
    n_pairs = q_ref.shape[-1] // LANES
    for p in range(n_pairs):
        ls = slice(p * LANES, (p + 1) * LANES)
        o2 = _head_pair_attention(q_ref[0, :, ls], [k_ref[0, :, ls]], [v_ref[0, :, ls]], [None])
        o_ref[0, :, ls] = o2.astype(BF16)


def _ctx_attn_call(q, k, v):
    b, s, cw = q.shape
    spec = pl.BlockSpec((1, s, cw), lambda bi: (bi, 0, 0))
    return pl.pallas_call(
        _ctx_attn_kernel,
        grid=(b,),
        in_specs=[spec, spec, spec],
        out_specs=spec,
        out_shape=jax.ShapeDtypeStruct((b, s, cw), BF16),
        compiler_params=_cparams(("parallel",), 8 * s * cw * 2 + 16 * s * s * 4),
        name="ctx_attn",
    )(q, k, v)


def _bias_tables(rpb_l, grid_rows):
    w = GRID_W
    n_heads, n_dy, n_dx = rpb_l.shape
    qc = np.arange(w)[:, None]
    kc = np.arange(w)[None, :]
    cs = np.clip(qc - NA_KW // 2, 0, w - NA_KW)
    ok_c = (kc >= cs) & (kc < cs + NA_KW)
    sel_x = ok_c[..., None] & ((kc - qc + NA_KW - 1)[..., None] == np.arange(n_dx))
    j = np.arange(Q_ROWS)[:, None]
    a = np.arange(BAND_ROWS)[None, :]
    sel_y, ok_r = [], []
    for r0, start in ((0, 0), (Q_ROWS, 0), (grid_rows - Q_ROWS, grid_rows - BAND_ROWS)):
        r = r0 + j
        rs = np.clip(r - NA_KH // 2, 0, grid_rows - NA_KH)
        kr = start + a
        ok = (kr >= rs) & (kr < rs + NA_KH)
        sel_y.append(ok[..., None] & ((kr - r + NA_KH - 1)[..., None] == np.arange(n_dy)))
        ok_r.append(ok)
    sel_y = jnp.asarray(np.stack(sel_y), F32)
    ok_r = jnp.asarray(np.stack(ok_r))
    cols = jnp.einsum("hyx,wcx->hywc", rpb_l, jnp.asarray(sel_x, F32), precision=lax.Precision.HIGHEST)
    table = jnp.einsum("tjay,hywc->thjwac", sel_y, cols, precision=lax.Precision.HIGHEST)
    valid = ok_r[:, None, :, None, :, None] & jnp.asarray(ok_c)[None, None, None, :, None, :]
    table = jnp.where(valid, table, NEG)
    return table.reshape(3, n_heads, Q_ROWS * w, BAND_ROWS * w)


def _merge_kernel(attn_ref, vcu_ref, vprev_ref, vnext_ref, bg_ref, x_ref, wout_ref, convw_ref,
                  gout_ref, gpost_ref, gt_ref, gpre_ref, sc_ref, sh_ref, *rest, n_exp):
    moe = n_exp > 0
    i = pl.program_id(1)
    n = pl.num_programs(1)
    v = vcu_ref[0].astype(F32)
    tm, cw = v.shape
    row = lax.broadcasted_iota(I32, v.shape, 0)
    prev = jnp.where(i > 0, vprev_ref[0, HALO - 1:HALO, :].astype(F32), 0.0)
    nxt = jnp.where(i < n - 1, vnext_ref[0, 0:1, :].astype(F32), 0.0)
    v_m1 = jnp.where(row == 0, prev, pltpu.roll(v, 1, 0))
    v_p1 = jnp.where(row == tm - 1, nxt, pltpu.roll(v, tm - 1, 0))
    w3 = convw_ref[...]
    conv = w3[0:1] * v_m1 + w3[1:2] * v + w3[2:3] * v_p1
    cv = bg_ref[0].astype(F32) * conv
    g_out = gout_ref[...]
    a = _rms(attn_ref[0].astype(F32), g_out[:, :cw])
    c = _rms(cv, g_out[:, cw:])
    y = _dot(a.astype(BF16), wout_ref[:cw, :]) + _dot(c.astype(BF16), wout_ref[cw:, :])
    xn = x_ref[0] + gt_ref[0] * _rms(y, gpost_ref[...])
    h = _rms(xn, gpre_ref[...]) * (1.0 + sc_ref[0]) + sh_ref[0]
    if not moe:
        x_out_ref, h_ref = rest
        x_out_ref[0] = xn
        h_ref[0] = h.astype(BF16)
        return
    wr_ref, x_out_ref, h_ref, ti_ref, tw_ref = rest
    x_out_ref[0] = xn
    h_ref[0] = h
    hh, hl = _split_bf16(h)
    wh, wl = _split_bf16(wr_ref[...])
    both = _dot_nt(jnp.concatenate([wh, wl], axis=0), hh)
    logits = both[:EXPERT_ROWS] + (both[EXPERT_ROWS:] + _dot_nt(wh, hl))
    sub = lax.broadcasted_iota(I32, logits.shape, 0).astype(F32)
    logits = jnp.where(sub < n_exp, logits, NEG)
    big = float(EXPERT_ROWS)
    m1 = logits.max(axis=0, keepdims=True)
    i1 = jnp.where(logits == m1, sub, big).min(axis=0, keepdims=True)
    l2 = jnp.where(sub == i1, NEG, logits)
    m2 = l2.max(axis=0, keepdims=True)
    i2 = jnp.where(l2 == m2, sub, big).min(axis=0, keepdims=True)
    e = jnp.exp(m2 - m1)
    w1 = 1.0 / (1.0 + e)
    ti_ref[0] = jnp.concatenate([i1, i2], axis=0).astype(I32)
    tw_ref[0] = jnp.concatenate([w1, e * w1], axis=0)


def _merge_call(attn, vcu, bg, x, wout, convw, gout, gpost, gt, gpre, sc, sh, w_router, name):
    b, s, d = x.shape
    cw = attn.shape[-1]
    tm = min(TOKEN_TILE, s)
    nt = s // tm
    moe = w_router is not None
    n_exp = w_router.shape[-1] if moe else 0
    if moe:
        wr = jnp.zeros((EXPERT_ROWS, d), F32).at[:n_exp].set(w_router.T)
    per_batch = gt.shape[0] > 1
    hb = tm // HALO
    n_halo = s // HALO
    tile = lambda width: pl.BlockSpec((1, tm, width), lambda bi, i: (bi, i, 0))
    vec = lambda width: pl.BlockSpec((1, width), lambda bi, i: (0, 0))
    mod = pl.BlockSpec((1, 1, d), (lambda bi, i: (bi, 0, 0)) if per_batch else (lambda bi, i: (0, 0, 0)))
    in_specs = [tile(cw), tile(cw),
                pl.BlockSpec((1, HALO, cw), lambda bi, i: (bi, jnp.maximum(i * hb - 1, 0), 0)),
                pl.BlockSpec((1, HALO, cw), lambda bi, i: (bi, jnp.minimum((i + 1) * hb, n_halo - 1), 0)),
                tile(cw), tile(d),
                pl.BlockSpec(wout.shape, lambda bi, i: (0, 0)),
                pl.BlockSpec(convw.shape, lambda bi, i: (0, 0)),
                vec(d), vec(d), mod, vec(d), mod, mod]
    args = [attn, vcu, vcu, vcu, bg, x, wout, convw, gout, gpost, gt, gpre, sc, sh]
    out_specs = [tile(d), tile(d)]
    out_shape = [jax.ShapeDtypeStruct((b, s, d), F32),
                 jax.ShapeDtypeStruct((b, s, d), F32 if moe else BF16)]
    if moe:
        in_specs.append(pl.BlockSpec(wr.shape, lambda bi, i: (0, 0)))
        args.append(wr)
        route = pl.BlockSpec((1, TOP_K, tm), lambda bi, i: (bi, 0, i))
        out_specs += [route, route]
        out_shape += [jax.ShapeDtypeStruct((b, TOP_K, s), I32), jax.ShapeDtypeStruct((b, TOP_K, s), F32)]
    vmem = 2 * wout.size * 2 + 2 * tm * (3 * cw * 2 + 3 * d * 4) + 16 * tm * d * 4
    return pl.pallas_call(
        functools.partial(_merge_kernel, n_exp=n_exp),
        grid=(b, nt),
        in_specs=in_specs,
        out_specs=out_specs,
        out_shape=out_shape,
        compiler_params=_cparams(("parallel", "arbitrary"), vmem),
        name=name,
    )(*args)


def _swiglu(hb, wgu, wdn, ff):
    chunk = min(FF_CHUNK, ff)
    acc = None
    for c in range(ff // chunk):
        lo, hi = c * chunk, (c + 1) * chunk
        g = _dot(hb, wgu(lo, hi))
        u = _dot(hb, wgu(ff + lo, ff + hi))
        part = _dot((_silu(g) * u).astype(BF16), wdn(lo, hi))
        acc = part if acc is None else acc + part
    return acc


def _ffn_kernel(h_ref, x_ref, wgu_ref, wdn_ref, gpost_ref, gt_ref, o_ref):
    ff = wdn_ref.shape[0]
    y = _swiglu(h_ref[0], lambda lo, hi: wgu_ref[:, lo:hi], lambda lo, hi: wdn_ref[lo:hi, :], ff)
    o_ref[0] = x_ref[0] + gt_ref[0] * _rms(y, gpost_ref[...])


def _ffn_call(h, x, wgu, wdn, gpost, gt, name):
    b, s, d = x.shape
    tm = min(TOKEN_TILE, s)
    per_batch = gt.shape[0] > 1
    tile = pl.BlockSpec((1, tm, d), lambda bi, i: (bi, i, 0))
    vmem = (wgu.size + wdn.size) * 2 + 2 * tm * d * (2 + 4 + 4) + 10 * tm * d * 4
    return pl.pallas_call(
        _ffn_kernel,
        grid=(b, s // tm),
        in_specs=[tile, tile,
                  pl.BlockSpec(wgu.shape, lambda bi, i: (0, 0)),
                  pl.BlockSpec(wdn.shape, lambda bi, i: (0, 0)),
                  pl.BlockSpec((1, d), lambda bi, i: (0, 0)),
                  pl.BlockSpec((1, 1, d), (lambda bi, i: (bi, 0, 0)) if per_batch else (lambda bi, i: (0, 0, 0)))],
        out_specs=tile,
        out_shape=jax.ShapeDtypeStruct((b, s, d), F32),
        compiler_params=_cparams(("parallel", "parallel"), vmem),
        name=name,
    )(h, x, wgu, wdn, gpost, gt)


def _plan_kernel(ti_ref, rank_ref, cnt_ref, carry):
    @pl.when(pl.program_id(0) == 0)
    def _():
        carry[...] = jnp.zeros_like(carry)

    ti = ti_ref[...]
    tm = ti.shape[1]
    sub = lax.broadcasted_iota(I32, (EXPERT_ROWS, tm), 0)
    e1 = sub == ti[0:1, :]
    e2 = sub == ti[1:2, :]
    c = e1.astype(F32) + e2.astype(F32)
    tri = (lax.broadcasted_iota(I32, (tm, tm), 0) < lax.broadcasted_iota(I32, (tm, tm), 1)).astype(BF16)
    before = _dot(c.astype(BF16), tri) + carry[:, 0:1]
    r1 = jnp.where(e1, before, 0.0).sum(axis=0, keepdims=True)
    r2 = jnp.where(e2, before, 0.0).sum(axis=0, keepdims=True)
    rank_ref[...] = jnp.concatenate([r1, r2], axis=0).astype(I32)
    carry[...] += c.sum(axis=1, keepdims=True)
    cnt_ref[...] = carry[...]


def _plan_call(ti):
    n = ti.shape[1]
    tm = min(TOKEN_TILE, n)
    return pl.pallas_call(
        _plan_kernel,
        grid=(n // tm,),
        in_specs=[pl.BlockSpec((TOP_K, tm), lambda j: (0, j))],
        out_specs=[pl.BlockSpec((TOP_K, tm), lambda j: (0, j)),
                   pl.BlockSpec((EXPERT_ROWS, LANES), lambda j: (0, 0))],
        out_shape=[jax.ShapeDtypeStruct((TOP_K, n), I32), jax.ShapeDtypeStruct((EXPERT_ROWS, LANES), F32)],
        scratch_shapes=[pltpu.VMEM((EXPERT_ROWS, LANES), F32)],
        compiler_params=_cparams(("arbitrary",), 8 * tm * tm * 4 + 16 * tm * LANES * 4),
        name="route_plan",
    )(ti)


def _row_copy(src, src_row, dst, dst_row, sem):
    return pltpu.make_async_copy(src.at[pl.ds(src_row, 1), :], dst.at[pl.ds(dst_row, 1), :], sem)


def _dispatch_kernel(p1_ref, p2_ref, h_ref, init_ref, o_ref, sem):
    del init_ref
    tm = h_ref.shape[0]
    base = pl.program_id(0) * tm

    def issue(t, carry):
        _row_copy(h_ref, t, o_ref, p1_ref[base + t], sem).start()
        _row_copy(h_ref, t, o_ref, p2_ref[base + t], sem).start()
        return carry

    def drain(t, carry):
        _row_copy(h_ref, 0, o_ref, 0, sem).wait()
        _row_copy(h_ref, 0, o_ref, 0, sem).wait()
        return carry

    lax.fori_loop(0, tm, issue, 0, unroll=ROW_DMA_UNROLL)
    lax.fori_loop(0, tm, drain, 0, unroll=ROW_DMA_UNROLL)


def _dispatch_call(pos1, pos2, h, n_rows):
    n, d = h.shape
    tm = min(ROUTE_TILE, n)
    init = jnp.zeros((n_rows, d), F32)
    return pl.pallas_call(
        _dispatch_kernel,
        grid_spec=pltpu.PrefetchScalarGridSpec(
            num_scalar_prefetch=2,
            grid=(n // tm,),
            in_specs=[pl.BlockSpec((tm, d), lambda j, p1, p2: (j, 0)),
                      pl.BlockSpec(memory_space=pl.ANY)],
            out_specs=pl.BlockSpec(memory_space=pl.ANY),
            scratch_shapes=[pltpu.SemaphoreType.DMA(())]),
        out_shape=jax.ShapeDtypeStruct((n_rows, d), F32),
        input_output_aliases={3: 0},
        compiler_params=_cparams(("arbitrary",), 4 * tm * d * 4),
        name="route_dispatch",
    )(pos1, pos2, h, init)


def _experts_kernel(te_ref, tv_ref, h_ref, wgu_ref, wdn_ref, y_ref):
    j = pl.program_id(0)
    ff = wdn_ref.shape[1]

    @pl.when(tv_ref[j] > 0)
    def _():
        y_ref[...] = _swiglu(h_ref[...].astype(BF16),
                             lambda lo, hi: wgu_ref[0, :, lo:hi],
                             lambda lo, hi: wdn_ref[0, lo:hi, :], ff)

    @pl.when(tv_ref[j] == 0)
    def _():
        y_ref[...] = jnp.zeros_like(y_ref)


def _experts_call(tile_expert, tile_valid, h_sorted, wgu, wdn, tm):
    n_rows, d = h_sorted.shape
    n_exp, _, ff2 = wgu.shape
    ff = wdn.shape[1]
    vmem = (wgu.size + wdn.size) // n_exp * 2 + 4 * tm * d * 4 + 10 * tm * d * 4
    return pl.pallas_call(
        _experts_kernel,
        grid_spec=pltpu.PrefetchScalarGridSpec(
            num_scalar_prefetch=2,
            grid=(n_rows // tm,),
            in_specs=[pl.BlockSpec((tm, d), lambda j, te, tv: (j, 0)),
                      pl.BlockSpec((1, d, ff2), lambda j, te, tv: (te[j], 0, 0),
                                   pipeline_mode=pl.Buffered(1)),
                      pl.BlockSpec((1, ff, d), lambda j, te, tv: (te[j], 0, 0),
                                   pipeline_mode=pl.Buffered(1))],
            out_specs=pl.BlockSpec((tm, d), lambda j, te, tv: (j, 0))),
        out_shape=jax.ShapeDtypeStruct((n_rows, d), F32),
        compiler_params=_cparams(("arbitrary",), vmem),
        name="route_experts",
    )(tile_expert, tile_valid, h_sorted, wgu, wdn)


def _combine_kernel(p1_ref, p2_ref, tw_ref, x_ref, gt_ref, gpost_ref, y_ref, o_ref, buf, sem):
    tm = x_ref.shape[0]
    base = pl.program_id(0) * tm

    def issue(t, carry):
        _row_copy(y_ref, p1_ref[base + t], buf.at[0], t, sem).start()
        _row_copy(y_ref, p2_ref[base + t], buf.at[1], t, sem).start()
        return carry

    def drain(t, carry):
        _row_copy(y_ref, 0, buf.at[0], 0, sem).wait()
        _row_copy(y_ref, 0, buf.at[1], 0, sem).wait()
        return carry

    lax.fori_loop(0, tm, issue, 0, unroll=ROW_DMA_UNROLL)
    lax.fori_loop(0, tm, drain, 0, unroll=ROW_DMA_UNROLL)
    tw = tw_ref[...]
    y = tw[:, 0:1] * buf[0] + tw[:, 1:2] * buf[1]
    o_ref[...] = x_ref[...] + gt_ref[0] * _rms(y, gpost_ref[...])


def _combine_call(pos1, pos2, tw, x, gt, gpost, y_sorted, seq):
    n, d = x.shape
    tm = min(ROUTE_TILE, seq)
    per_seq = seq // tm
    return pl.pallas_call(
        _combine_kernel,
        grid_spec=pltpu.PrefetchScalarGridSpec(
            num_scalar_prefetch=2,
            grid=(n // tm,),
            in_specs=[pl.BlockSpec((tm, TOP_K), lambda j, p1, p2: (j, 0)),
                      pl.BlockSpec((tm, d), lambda j, p1, p2: (j, 0)),
                      pl.BlockSpec((1, 1, d), lambda j, p1, p2: (j // per_seq, 0, 0)),
                      pl.BlockSpec((1, d), lambda j, p1, p2: (0, 0)),
                      pl.BlockSpec(memory_space=pl.ANY)],
            out_specs=pl.BlockSpec((tm, d), lambda j, p1, p2: (j, 0)),
            scratch_shapes=[pltpu.VMEM((TOP_K, tm, d), F32), pltpu.SemaphoreType.DMA(())]),
        out_shape=jax.ShapeDtypeStruct((n, d), F32),
        compiler_params=_cparams(("arbitrary",), 12 * tm * d * 4),
        name="route_combine",
    )(pos1, pos2, tw, x, gt, gpost, y_sorted)


def _moe_call(h, ti, tw, x, gt, gpost, wgu, wdn):
    b, s, d = x.shape
    n = b * s
    n_exp = wgu.shape[0]
    tm = TOKEN_TILE
    ti = ti.transpose(1, 0, 2).reshape(TOP_K, n)
    rank, cnt = _plan_call(ti)
    counts = cnt[:n_exp, 0].astype(I32)
    padded = (counts + tm - 1) // tm * tm
    ends = jnp.cumsum(padded)
    starts = ends - padded
    experts = jnp.arange(n_exp, dtype=I32)
    pos = rank + jnp.sum(jnp.where(ti[..., None] == experts, starts, 0), axis=-1)
    n_tiles = n * TOP_K // tm + n_exp
    row0 = jnp.arange(n_tiles, dtype=I32) * tm
    tile_expert = jnp.minimum(jnp.sum(row0[:, None] >= ends[None, :], axis=1), n_exp - 1).astype(I32)
    tile_valid = (row0 < ends[-1]).astype(I32)
    pos1, pos2 = pos[0], pos[1]
    h_sorted = _dispatch_call(pos1, pos2, h.reshape(n, d), n_tiles * tm)
    y_sorted = _experts_call(tile_expert, tile_valid, h_sorted, wgu, wdn, tm)
    tw = tw.transpose(0, 2, 1).reshape(n, TOP_K)
    out = _combine_call(pos1, pos2, tw, x.reshape(n, d), gt, gpost, y_sorted, s)
    return out.reshape(b, s, d)


def kernel(x, c, ctx, c_ctx, w_ada, b_ada, norm_g, w_in, rpb, conv_w, out_norm_g, w_out,
           w_gu_dense, w_down_dense, w_router, w_gu_moe, w_down_moe):
    depth = w_ada.shape[0]
    b, s, d = x.shape
    cw = d // 2

    cc =jnp.zeros((16, d), F32).at[:b].set(c).at[b].set(c_ctx)
    mods = _ada_call(cc, w_ada, b_ada)

    for l in range(depth):
        last = l == depth - 1
        m = mods[l].reshape(16, 6, d)
        lat = [m[:b, j][:, None, :] for j in range(6)]
        cxm = [m[b:b + 1, j][:, None, :] for j in range(6)]
        g = [norm_g[l, j][None, :] for j in range(4)]
        w_in_l = w_in[l].astype(BF16)
        w_out_l = w_out[l].astype(BF16)
        gout = out_norm_g[l][None, :]

        q, k, v, bg, vcu = _inproj_call(x, g[0], lat[1], lat[0], w_in_l, True, f"inproj_x{l}")
        if last:
            kc, vc = _inproj_call(ctx, g[0], cxm[1], cxm[0], w_in_l[:, cw:3 * cw], False, f"inproj_c{l}")
        else:
            qc, kc, vc, bgc, vcuc = _inproj_call(ctx, g[0], cxm[1], cxm[0], w_in_l, True, f"inproj_c{l}")

        attn = _na_call(q, k, v, kc, vc, _bias_tables(rpb[l], s // GRID_W))
        moe = l % 2 == 1
        x, h, *route = _merge_call(attn, vcu, bg, x, w_out_l, conv_w[l], gout, g[1], lat[2], g[2],
                                   lat[4], lat[3], w_router[l // 2] if moe else None, f"merge_x{l}")
        if not last:
            attn_c = _ctx_attn_call(qc, kc, vc)
            ctx, hc = _merge_call(attn_c, vcuc, bgc, ctx, w_out_l, conv_w[l], gout, g[1], cxm[2], g[2],
                                  cxm[4], cxm[3], None, f"merge_c{l}")

        if moe:
            x = _moe_call(h, route[0], route[1], x, lat[5], g[3],
                          w_gu_moe[l // 2].astype(BF16), w_down_moe[l // 2].astype(BF16))
        else:
            wgu = w_gu_dense[l // 2].astype(BF16)
            wdn = w_down_dense[l // 2].astype(BF16)
            x = _ffn_call(h, x, wgu, wdn, g[3], lat[5], f"ffn_x{l}")
            if not last:
                ctx = _ffn_call(hc, ctx, wgu, wdn, g[3], cxm[5], f"ffn_c{l}")
    return x
```

```python
import functools

import numpy as np
import jax
import jax.numpy as jnp
from jax import lax
from jax.experimental import pallas as pl
from jax.experimental.pallas import tpu as pltpu

F32 = jnp.float32
BF16 = jnp.bfloat16
I32 = jnp.int32

EPS = 1e-6
GRID_W = 64
HEAD_DIM = 64
NA_KH = 8
NA_KW = 16
TOP_K = 2
LANES = 128
Q_ROWS = 4
BAND_ROWS = 12
ATTN_BLOCKS_PER_STEP = 4
NEG = -1e30
HALO = 16
V7X_VMEM_CAP = 60 * 1024 * 1024
TOKEN_TILE = 512
FF_CHUNK = 512
EXPERT_ROWS = 16


def _cparams(sem, vmem_bytes):
    return pltpu.CompilerParams(dimension_semantics=sem,
                                vmem_limit_bytes=int(min(vmem_bytes, V7X_VMEM_CAP)))


def _rms(x, g):
    return x * lax.rsqrt(jnp.mean(x * x, axis=-1, keepdims=True) + EPS) * g


def _dot(a, b):
    return jnp.dot(a, b, preferred_element_type=F32)


def _dot_nt(a, b):
    return lax.dot_general(a, b, (((1,), (1,)), ((), ())), preferred_element_type=F32)


def _split_bf16(a):
    hi = a.astype(BF16)
    lo = (a - hi.astype(F32)).astype(BF16)
    return hi, lo


def _dot3(a, w):
    ah, al = _split_bf16(a)
    wh, wl = _split_bf16(w)
    return _dot(ah, wh) + (_dot(ah, wl) + _dot(al, wh))


def _silu(x):
    return x / (1.0 + jnp.exp(-x))


def _ada_kernel(c_ref, w_ref, b_ref, o_ref):
    o_ref[0] = _dot3(_silu(c_ref[...]), w_ref[0]) + b_ref[0]


def _ada_call(cc, w_ada, b_ada):
    depth, d, n = w_ada.shape
    rows = cc.shape[0]
    tn = n // 4
    return pl.pallas_call(
        _ada_kernel,
        grid=(depth, n // tn),
        in_specs=[pl.BlockSpec((rows, d), lambda l, j: (0, 0)),
                  pl.BlockSpec((1, d, tn), lambda l, j: (l, 0, j)),
                  pl.BlockSpec((1, 1, tn), lambda l, j: (l, 0, j))],
        out_specs=pl.BlockSpec((1, rows, tn), lambda l, j: (l, 0, j)),
        out_shape=jax.ShapeDtypeStruct((depth, rows, n), F32),
        compiler_params=_cparams(("parallel", "parallel"), 6 * d * tn * 4),
        name="adaln",
    )(cc, w_ada, b_ada.reshape(depth, 1, n))


def _inproj_kernel(x_ref, g_ref, sc_ref, sh_ref, w_ref, *out_refs, full):
    h = _rms(x_ref[0], g_ref[...]) * (1.0 + sc_ref[0]) + sh_ref[0]
    hb = h.astype(BF16)
    cw = out_refs[0].shape[-1]

    def col(j):
        return _dot(hb, w_ref[:, j * cw:(j + 1) * cw])

    if full:
        q_ref, k_ref, v_ref, bg_ref, vcu_ref = out_refs
        q_ref[0] = col(0).astype(BF16)
        k_ref[0] = col(1).astype(BF16)
        v_ref[0] = col(2).astype(BF16)
        bg_ref[0] = col(3).astype(BF16)
        vcu_ref[0] = (col(4) * col(5)).astype(BF16)
    else:
        k_ref, v_ref = out_refs
        k_ref[0] = col(0).astype(BF16)
        v_ref[0] = col(1).astype(BF16)


def _inproj_call(x, g, sc, sh, w, full, name):
    b, s, d = x.shape
    cw = d // 2
    tm = min(TOKEN_TILE, s)
    n_out = 5 if full else 2
    per_batch = sc.shape[0] > 1
    mod_spec = pl.BlockSpec((1, 1, d), (lambda bi, i: (bi, 0, 0)) if per_batch else (lambda bi, i: (0, 0, 0)))
    out_spec = pl.BlockSpec((1, tm, cw), lambda bi, i: (bi, i, 0))
    vmem = 2 * tm * d * 4 + 2 * w.size * 2 + 2 * n_out * tm * cw * 2 + 8 * tm * d * 4
    return pl.pallas_call(
        functools.partial(_inproj_kernel, full=full),
        grid=(b, s // tm),
        in_specs=[pl.BlockSpec((1, tm, d), lambda bi, i: (bi, i, 0)),
                  pl.BlockSpec((1, d), lambda bi, i: (0, 0)),
                  mod_spec, mod_spec,
                  pl.BlockSpec(w.shape, lambda bi, i: (0, 0))],
        out_specs=[out_spec] * n_out,
        out_shape=[jax.ShapeDtypeStruct((b, s, cw), BF16)] * n_out,
        compiler_params=_cparams(("parallel", "parallel"), vmem),
        name=name,
    )(x, g, sc, sh, w)


def _head_pair_attention(q2, key_parts, value_parts, bias_parts):
    m_rows = q2.shape[0]
    lane = lax.broadcasted_iota(I32, q2.shape, 1)
    qs = q2 * (HEAD_DIM ** -0.5)
    qq = jnp.concatenate([jnp.where(lane < HEAD_DIM, qs, 0), jnp.where(lane >= HEAD_DIM, qs, 0)], axis=0)
    scores = []
    for kk, bias in zip(key_parts, bias_parts):
        s = _dot_nt(qq, kk)
        scores.append(s if bias is None else s + bias)
    m = scores[0].max(axis=-1, keepdims=True)
    for s in scores[1:]:
        m = jnp.maximum(m, s.max(axis=-1, keepdims=True))
    acc = None
    for s, vv in zip(scores, value_parts):
        v_aug = jnp.concatenate([vv, jnp.ones_like(vv)], axis=1)
        acc_i = _dot(jnp.exp(s - m).astype(BF16), v_aug)
        acc = acc_i if acc is None else acc + acc_i
    out = acc[:, :LANES] / acc[:, LANES:]
    return jnp.where(lane < HEAD_DIM, out[:m_rows], out[m_rows:])


def _na_kernel(q_ref, k_ref, v_ref, kc_ref, vc_ref, bias_ref, o_ref, *, grid_rows, blocks_per_step):
    tq = Q_ROWS * GRID_W
    band = BAND_ROWS * GRID_W
    n_blocks = grid_rows // Q_ROWS
    n_pairs = q_ref.shape[-1] // LANES
    for jb in range(blocks_per_step):
        ib = pl.program_id(1) * blocks_per_step + jb
        start_row = jnp.clip(Q_ROWS * ib - NA_KH // 2, 0, grid_rows - BAND_ROWS)
        tok0 = pl.multiple_of(start_row * GRID_W, GRID_W)
        kind = jnp.where(ib == 0, 0, jnp.where(ib == n_blocks - 1, 2, 1))
        rows = slice(jb * tq, (jb + 1) * tq)
        for p in range(n_pairs):
            ls = slice(p * LANES, (p + 1) * LANES)
            k2 = k_ref[0, pl.ds(tok0, band), ls]
            v2 = v_ref[0, pl.ds(tok0, band), ls]
            o2 = _head_pair_attention(q_ref[0, rows, ls], [k2, kc_ref[0, :, ls]], [v2, vc_ref[0, :, ls]],
                                      [bias_ref[kind, p], None])
            o_ref[0, rows, ls] = o2.astype(BF16)


def _na_call(q, k, v, kc, vc, bias):
    b, s, cw = q.shape
    ctx = kc.shape[1]
    grid_rows = s // GRID_W
    tq = Q_ROWS * GRID_W
    nblk = s // tq
    bps = min(ATTN_BLOCKS_PER_STEP, nblk)
    band = BAND_ROWS * GRID_W
    n_heads = cw // HEAD_DIM
    bias = bias.reshape(3, n_heads // 2, 2 * tq, band)
    vmem = (4 * s * cw * 2 + 4 * ctx * cw * 2 + bias.size * 4
            + 4 * bps * tq * cw * 2 + 16 * tq * (band + ctx) * 4)
    return pl.pallas_call(
        functools.partial(_na_kernel, grid_rows=grid_rows, blocks_per_step=bps),
        grid=(b, nblk // bps),
        in_specs=[pl.BlockSpec((1, bps * tq, cw), lambda bi, i: (bi, i, 0)),
                  pl.BlockSpec((1, s, cw), lambda bi, i: (bi, 0, 0)),
                  pl.BlockSpec((1, s, cw), lambda bi, i: (bi, 0, 0)),
                  pl.BlockSpec((1, ctx, cw), lambda bi, i: (bi, 0, 0)),
                  pl.BlockSpec((1, ctx, cw), lambda bi, i: (bi, 0, 0)),
                  pl.BlockSpec(bias.shape, lambda bi, i: (0, 0, 0, 0))],
        out_specs=pl.BlockSpec((1, bps * tq, cw), lambda bi, i: (bi, i, 0)),
        out_shape=jax.ShapeDtypeStruct((b, s, cw), BF16),
        compiler_params=_cparams(("parallel", "arbitrary"), vmem),
        name="neigh_attn",
    )(q, k, v, kc, vc, bias)


def _ctx_attn_kernel(q_ref, k_ref, v_ref, o_ref):
    n_pairs = q_ref.shape[-1] // LANES
    for p in range(n_pairs):
        ls = slice(p * LANES, (p + 1) * LANES)
        o2 = _head_pair_attention(q_ref[0, :, ls], [k_ref[0, :, ls]], [v_ref[0, :, ls]], [None])
        o_ref[0, :, ls] = o2.astype(BF16)


def _ctx_attn_call(q, k, v):
    b, s, cw = q.shape
    spec = pl.BlockSpec((1, s, cw), lambda bi: (bi, 0, 0))
    return pl.pallas_call(
        _ctx_attn_kernel,
        grid=(b,),
        in_specs=[spec, spec, spec],
        out_specs=spec,
        out_shape=jax.ShapeDtypeStruct((b, s, cw), BF16),
        compiler_params=_cparams(("parallel",), 8 * s * cw * 2 + 16 * s * s * 4),
        name="ctx_attn",
    )(q, k, v)


def _bias_tables(rpb_l, grid_rows):
    w = GRID_W
    n_heads, n_dy, n_dx = rpb_l.shape
    qc = np.arange(w)[:, None]
    kc = np.arange(w)[None, :]
    cs = np.clip(qc - NA_KW // 2, 0, w - NA_KW)
    ok_c = (kc >= cs) & (kc < cs + NA_KW)
    sel_x = ok_c[..., None] & ((kc - qc + NA_KW - 1)[..., None] == np.arange(n_dx))
    j = np.arange(Q_ROWS)[:, None]
    a = np.arange(BAND_ROWS)[None, :]
    sel_y, ok_r = [], []
    for r0, start in ((0, 0), (Q_ROWS, 0), (grid_rows - Q_ROWS, grid_rows - BAND_ROWS)):
        r = r0 + j
        rs = np.clip(r - NA_KH // 2, 0, grid_rows - NA_KH)
        kr = start + a
        ok = (kr >= rs) & (kr < rs + NA_KH)
        sel_y.append(ok[..., None] & ((kr - r + NA_KH - 1)[..., None] == np.arange(n_dy)))
        ok_r.append(ok)
    sel_y = jnp.asarray(np.stack(sel_y), F32)
    ok_r = jnp.asarray(np.stack(ok_r))
    cols = jnp.einsum("hyx,wcx->hywc", rpb_l, jnp.asarray(sel_x, F32), precision=lax.Precision.HIGHEST)
    table = jnp.einsum("tjay,hywc->thjwac", sel_y, cols, precision=lax.Precision.HIGHEST)
    valid = ok_r[:, None, :, None, :, None] & jnp.asarray(ok_c)[None, None, None, :, None, :]
    table = jnp.where(valid, table, NEG)
    return table.reshape(3, n_heads, Q_ROWS * w, BAND_ROWS * w)


def _merge_kernel(attn_ref, vcu_ref, vprev_ref, vnext_ref, bg_ref, x_ref, wout_ref, convw_ref,
                  gout_ref, gpost_ref, gt_ref, gpre_ref, sc_ref, sh_ref, *rest, n_exp):
    moe = n_exp > 0
    i = pl.program_id(1)
    n = pl.num_programs(1)
    v = vcu_ref[0].astype(F32)
    tm, cw = v.shape
    row = lax.broadcasted_iota(I32, v.shape, 0)
    prev = jnp.where(i > 0, vprev_ref[0, HALO - 1:HALO, :].astype(F32), 0.0)
    nxt = jnp.where(i < n - 1, vnext_ref[0, 0:1, :].astype(F32), 0.0)
    v_m1 = jnp.where(row == 0, prev, pltpu.roll(v, 1, 0))
    v_p1 = jnp.where(row == tm - 1, nxt, pltpu.roll(v, tm - 1, 0))
    w3 = convw_ref[...]
    conv = w3[0:1] * v_m1 + w3[1:2] * v + w3[2:3] * v_p1
    cv = bg_ref[0].astype(F32) * conv
    g_out = gout_ref[...]
    a = _rms(attn_ref[0].astype(F32), g_out[:, :cw])
    c = _rms(cv, g_out[:, cw:])
    y = _dot(a.astype(BF16), wout_ref[:cw, :]) + _dot(c.astype(BF16), wout_ref[cw:, :])
    xn = x_ref[0] + gt_ref[0] * _rms(y, gpost_ref[...])
    h = _rms(xn, gpre_ref[...]) * (1.0 + sc_ref[0]) + sh_ref[0]
    if not moe:
        x_out_ref, h_ref = rest
        x_out_ref[0] = xn
        h_ref[0] = h.astype(BF16)
        return
    wr_ref, x_out_ref, h_ref, ti_ref, tw_ref = rest
    x_out_ref[0] = xn
    h_ref[0] = h
    hh, hl = _split_bf16(h)
    wh, wl = _split_bf16(wr_ref[...])
    both = _dot_nt(jnp.concatenate([wh, wl], axis=0), hh)
    logits = both[:EXPERT_ROWS] + (both[EXPERT_ROWS:] + _dot_nt(wh, hl))
    sub = lax.broadcasted_iota(I32, logits.shape, 0).astype(F32)
    logits = jnp.where(sub < n_exp, logits, NEG)
    big = float(EXPERT_ROWS)
    m1 = logits.max(axis=0, keepdims=True)
    i1 = jnp.where(logits == m1, sub, big).min(axis=0, keepdims=True)
    l2 = jnp.where(sub == i1, NEG, logits)
    m2 = l2.max(axis=0, keepdims=True)
    i2 = jnp.where(l2 == m2, sub, big).min(axis=0, keepdims=True)
    e = jnp.exp(m2 - m1)
    w1 = 1.0 / (1.0 + e)
    ti_ref[0] = jnp.concatenate([i1, i2], axis=0).astype(I32)
    tw_ref[0] = jnp.concatenate([w1, e * w1], axis=0)


def _merge_call(attn, vcu, bg, x, wout, convw, gout, gpost, gt, gpre, sc, sh, w_router, name):
    b, s, d = x.shape
    cw = attn.shape[-1]
    tm = min(TOKEN_TILE, s)
    nt = s // tm
    moe = w_router is not None
    n_exp = w_router.shape[-1] if moe else 0
    if moe:
        wr = jnp.zeros((EXPERT_ROWS, d), F32).at[:n_exp].set(w_router.T)
    per_batch = gt.shape[0] > 1
    hb = tm // HALO
    n_halo = s // HALO
    tile = lambda width: pl.BlockSpec((1, tm, width), lambda bi, i: (bi, i, 0))
    vec = lambda width: pl.BlockSpec((1, width), lambda bi, i: (0, 0))
    mod = pl.BlockSpec((1, 1, d), (lambda bi, i: (bi, 0, 0)) if per_batch else (lambda bi, i: (0, 0, 0)))
    in_specs = [tile(cw), tile(cw),
                pl.BlockSpec((1, HALO, cw), lambda bi, i: (bi, jnp.maximum(i * hb - 1, 0), 0)),
                pl.BlockSpec((1, HALO, cw), lambda bi, i: (bi, jnp.minimum((i + 1) * hb, n_halo - 1), 0)),
                tile(cw), tile(d),
                pl.BlockSpec(wout.shape, lambda bi, i: (0, 0)),
                pl.BlockSpec(convw.shape, lambda bi, i: (0, 0)),
                vec(d), vec(d), mod, vec(d), mod, mod]
    args = [attn, vcu, vcu, vcu, bg, x, wout, convw, gout, gpost, gt, gpre, sc, sh]
    out_specs = [tile(d), tile(d)]
    out_shape = [jax.ShapeDtypeStruct((b, s, d), F32),
                 jax.ShapeDtypeStruct((b, s, d), F32 if moe else BF16)]
    if moe:
        in_specs.append(pl.BlockSpec(wr.shape, lambda bi, i: (0, 0)))
        args.append(wr)
        route = pl.BlockSpec((1, TOP_K, tm), lambda bi, i: (bi, 0, i))
        out_specs += [route, route]
        out_shape += [jax.ShapeDtypeStruct((b, TOP_K, s), I32), jax.ShapeDtypeStruct((b, TOP_K, s), F32)]
    vmem = 2 * wout.size * 2 + 2 * tm * (3 * cw * 2 + 3 * d * 4) + 16 * tm * d * 4
    return pl.pallas_call(
        functools.partial(_merge_kernel, n_exp=n_exp),
        grid=(b, nt),
        in_specs=in_specs,
        out_specs=out_specs,
        out_shape=out_shape,
        compiler_params=_cparams(("parallel", "arbitrary"), vmem),
        name=name,
    )(*args)


def _swiglu(hb, wgu, wdn, ff, before_chunk=None):
    chunk = min(FF_CHUNK, ff)
    n_chunks = ff // chunk
    acc = None
    for c in range(n_chunks):
        if before_chunk is not None:
            before_chunk(c, n_chunks)
        lo, hi = c * chunk, (c + 1) * chunk
        g = _dot(hb, wgu(lo, hi))
        u = _dot(hb, wgu(ff + lo, ff + hi))
        part = _dot((_silu(g) * u).astype(BF16), wdn(lo, hi))
        acc = part if acc is None else acc + part
    return acc


def _ffn_kernel(h_ref, x_ref, wgu_ref, wdn_ref, gpost_ref, gt_ref, o_ref):
    ff = wdn_ref.shape[0]
    y = _swiglu(h_ref[0], lambda lo, hi: wgu_ref[:, lo:hi], lambda lo, hi: wdn_ref[lo:hi, :], ff)
    o_ref[0] = x_ref[0] + gt_ref[0] * _rms(y, gpost_ref[...])


def _ffn_call(h, x, wgu, wdn, gpost, gt, name):
    b, s, d = x.shape
    tm = min(TOKEN_TILE, s)
    per_batch = gt.shape[0] > 1
    tile = pl.BlockSpec((1, tm, d), lambda bi, i: (bi, i, 0))
    vmem = (wgu.size + wdn.size) * 2 + 2 * tm * d * (2 + 4 + 4) + 10 * tm * d * 4
    return pl.pallas_call(
        _ffn_kernel,
        grid=(b, s // tm),
        in_specs=[tile, tile,
                  pl.BlockSpec(wgu.shape, lambda bi, i: (0, 0)),
                  pl.BlockSpec(wdn.shape, lambda bi, i: (0, 0)),
                  pl.BlockSpec((1, d), lambda bi, i: (0, 0)),
                  pl.BlockSpec((1, 1, d), (lambda bi, i: (bi, 0, 0)) if per_batch else (lambda bi, i: (0, 0, 0)))],
        out_specs=tile,
        out_shape=jax.ShapeDtypeStruct((b, s, d), F32),
        compiler_params=_cparams(("parallel", "parallel"), vmem),
        name=name,
    )(h, x, wgu, wdn, gpost, gt)


def _plan_kernel(ti_ref, rank_ref, cnt_ref, carry):
    @pl.when(pl.program_id(0) == 0)
    def _():
        carry[...] = jnp.zeros_like(carry)

    ti = ti_ref[...]
    tm = ti.shape[1]
    sub = lax.broadcasted_iota(I32, (EXPERT_ROWS, tm), 0)
    e1 = sub == ti[0:1, :]
    e2 = sub == ti[1:2, :]
    c = e1.astype(F32) + e2.astype(F32)
    tri = (lax.broadcasted_iota(I32, (tm, tm), 0) < lax.broadcasted_iota(I32, (tm, tm), 1)).astype(BF16)
    before = _dot(c.astype(BF16), tri) + carry[:, 0:1]
    r1 = jnp.where(e1, before, 0.0).sum(axis=0, keepdims=True)
    r2 = jnp.where(e2, before, 0.0).sum(axis=0, keepdims=True)
    rank_ref[...] = jnp.concatenate([r1, r2], axis=0).astype(I32)
    carry[...] += c.sum(axis=1, keepdims=True)
    cnt_ref[...] = carry[...]


def _plan_call(ti):
    n = ti.shape[1]
    tm = min(TOKEN_TILE, n)
    return pl.pallas_call(
        _plan_kernel,
        grid=(n // tm,),
        in_specs=[pl.BlockSpec((TOP_K, tm), lambda j: (0, j))],
        out_specs=[pl.BlockSpec((TOP_K, tm), lambda j: (0, j)),
                   pl.BlockSpec((EXPERT_ROWS, LANES), lambda j: (0, 0))],
        out_shape=[jax.ShapeDtypeStruct((TOP_K, n), I32), jax.ShapeDtypeStruct((EXPERT_ROWS, LANES), F32)],
        scratch_shapes=[pltpu.VMEM((EXPERT_ROWS, LANES), F32)],
        compiler_params=_cparams(("arbitrary",), 8 * tm * tm * 4 + 16 * tm * LANES * 4),
        name="route_plan",
    )(ti)


def _row_copy(src, src_row, dst, dst_row, sem):
    return pltpu.make_async_copy(src.at[pl.ds(src_row, 1), :], dst.at[pl.ds(dst_row, 1), :], sem)


def _experts_kernel(te_ref, src0_ref, src_ref, dst_ref, h_ref, wgu_ref, wdn_ref, y_ref, hbuf, ybuf, gsem, ssem):
    del te_ref
    j = pl.program_id(0)
    last = pl.num_programs(0) - 1
    tm = hbuf.shape[1]
    ff = wdn_ref.shape[1]
    slot = j % 2
    other = 1 - slot

    def gather(idx_ref, r, s):
        return _row_copy(h_ref, idx_ref[0, 0, r], hbuf.at[s], r, gsem.at[s])

    def scatter(r, s):
        return _row_copy(ybuf.at[s], r, y_ref, dst_ref[0, 0, r], ssem.at[s])

    @pl.when(j == 0)
    def _():
        ybuf[...] = jnp.zeros_like(ybuf)
        for r in range(tm):
            gather(src0_ref, r, 0).start()

    for r in range(tm):
        gather(src0_ref, 0, slot).wait()

    @pl.when(j > 0)
    def _():
        for r in range(tm):
            scatter(0, slot).wait()

    def move_rows(c, n_chunks):
        for r in range(c * tm // n_chunks, (c + 1) * tm // n_chunks):
            gather(src_ref, r, other).start()
            scatter(r, other).start()

    ybuf[slot] = _swiglu(hbuf[slot].astype(BF16),
                         lambda lo, hi: wgu_ref[0, :, lo:hi],
                         lambda lo, hi: wdn_ref[0, lo:hi, :], ff, before_chunk=move_rows)

    @pl.when(j == last)
    def _():
        for r in range(tm):
            gather(src_ref, 0, other).wait()
            scatter(0, other).wait()


def _experts_call(tile_expert, src, dst, h, wgu, wdn, tm, y_rows):
    n_tiles = src.shape[0]
    d = h.shape[1]
    n_exp, _, ff2 = wgu.shape
    ff = wdn.shape[1]
    last_tile = n_tiles - 1
    idx = lambda f: pl.BlockSpec((1, 1, tm), f, memory_space=pltpu.SMEM)
    vmem = (wgu.size + wdn.size) // n_exp * 2 + 4 * tm * d * 4 + 10 * tm * d * 4
    return pl.pallas_call(
        _experts_kernel,
        grid_spec=pltpu.PrefetchScalarGridSpec(
            num_scalar_prefetch=1,
            grid=(n_tiles + 1,),
            in_specs=[idx(lambda j, te: (jnp.minimum(j, last_tile), 0, 0)),
                      idx(lambda j, te: (jnp.minimum(j + 1, last_tile), 0, 0)),
                      idx(lambda j, te: (j, 0, 0)),
                      pl.BlockSpec(memory_space=pl.ANY),
                      pl.BlockSpec((1, d, ff2), lambda j, te: (te[j], 0, 0), pipeline_mode=pl.Buffered(1)),
                      pl.BlockSpec((1, ff, d), lambda j, te: (te[j], 0, 0), pipeline_mode=pl.Buffered(1))],
            out_specs=pl.BlockSpec(memory_space=pl.ANY),
            scratch_shapes=[pltpu.VMEM((2, tm, d), F32), pltpu.VMEM((2, tm, d), F32),
                            pltpu.SemaphoreType.DMA((2,)), pltpu.SemaphoreType.DMA((2,))]),
        out_shape=jax.ShapeDtypeStruct((y_rows, d), F32),
        compiler_params=_cparams(("arbitrary",), vmem),
        name="route_experts",
    )(tile_expert, src, src, dst, h, wgu, wdn)


def _combine_kernel(y_ref, tw_ref, x_ref, gt_ref, gpost_ref, o_ref):
    d = x_ref.shape[-1]
    tw = tw_ref[...]
    y = tw[:, 0:1] * y_ref[:, :d] + tw[:, 1:2] * y_ref[:, d:]
    o_ref[...] = x_ref[...] + gt_ref[0] * _rms(y, gpost_ref[...])


def _combine_call(y_pairs, tw, x, gt, gpost, seq):
    n, d = x.shape
    tm = min(TOKEN_TILE, seq)
    per_seq = seq // tm
    return pl.pallas_call(
        _combine_kernel,
        grid=(n // tm,),
        in_specs=[pl.BlockSpec((tm, TOP_K * d), lambda j: (j, 0)),
                  pl.BlockSpec((tm, TOP_K), lambda j: (j, 0)),
                  pl.BlockSpec((tm, d), lambda j: (j, 0)),
                  pl.BlockSpec((1, 1, d), lambda j: (j // per_seq, 0, 0)),
                  pl.BlockSpec((1, d), lambda j: (0, 0))],
        out_specs=pl.BlockSpec((tm, d), lambda j: (j, 0)),
        out_shape=jax.ShapeDtypeStruct((n, d), F32),
        compiler_params=_cparams(("parallel",), 16 * tm * d * 4),
        name="route_combine",
    )(y_pairs, tw, x, gt, gpost)


def _moe_call(h, ti, tw, x, gt, gpost, wgu, wdn):
    b, s, d = x.shape
    n = b * s
    n_exp = wgu.shape[0]
    tm = TOKEN_TILE
    ti = ti.transpose(1, 0, 2).reshape(TOP_K, n)
    rank, cnt = _plan_call(ti)
    counts = cnt[:n_exp, 0].astype(I32)
    padded = (counts + tm - 1) // tm * tm
    ends = jnp.cumsum(padded)
    starts = ends - padded
    experts = jnp.arange(n_exp, dtype=I32)
    pos = rank + jnp.sum(jnp.where(ti[..., None] == experts, starts, 0), axis=-1)
    n_pairs = n * TOP_K
    n_tiles = n_pairs // tm + n_exp
    row0 = jnp.arange(n_tiles + 1, dtype=I32) * tm
    tile_expert = jnp.minimum(jnp.sum(row0[:, None] >= ends[None, :], axis=1), n_exp - 1).astype(I32)
    pair_id = 2 * jnp.arange(n, dtype=I32)[None, :] + jnp.arange(TOP_K, dtype=I32)[:, None]
    inv = jnp.full((n_tiles * tm,), -1, I32).at[pos.reshape(-1)].set(pair_id.reshape(-1), unique_indices=True)
    is_pad = inv < 0
    spare = n_pairs + tm + jnp.cumsum(is_pad.astype(I32)) - 1
    src = jnp.where(is_pad, 0, inv >> 1).reshape(n_tiles, 1, tm)
    dst = jnp.concatenate([n_pairs + jnp.arange(tm, dtype=I32), jnp.where(is_pad, spare, inv)])
    y = _experts_call(tile_expert, src, dst.reshape(n_tiles + 1, 1, tm), h.reshape(n, d), wgu, wdn, tm,
                      n_pairs + tm + n_exp * tm)
    tw = tw.transpose(0, 2, 1).reshape(n, TOP_K)
    out = _combine_call(y.reshape(-1, TOP_K * d), tw, x.reshape(n, d), gt, gpost, s)
    return out.reshape(b, s, d)


def kernel(x, c, ctx, c_ctx, w_ada, b_ada, norm_g, w_in, rpb, conv_w, out_norm_g, w_out,
           w_gu_dense, w_down_dense, w_router, w_gu_moe, w_down_moe):
    depth = w_ada.shape[0]
    b, s, d = x.shape
    cw = d // 2

    cc =jnp.zeros((16, d), F32).at[:b].set(c).at[b].set(c_ctx)
    mods = _ada_call(cc, w_ada, b_ada)

    for l in range(depth):
        last = l == depth - 1
        m = mods[l].reshape(16, 6, d)
        lat = [m[:b, j][:, None, :] for j in range(6)]
        cxm = [m[b:b + 1, j][:, None, :] for j in range(6)]
        g = [norm_g[l, j][None, :] for j in range(4)]
        w_in_l = w_in[l].astype(BF16)
        w_out_l = w_out[l].astype(BF16)
        gout = out_norm_g[l][None, :]

        q, k, v, bg, vcu = _inproj_call(x, g[0], lat[1], lat[0], w_in_l, True, f"inproj_x{l}")
        if last:
            kc, vc = _inproj_call(ctx, g[0], cxm[1], cxm[0], w_in_l[:, cw:3 * cw], False, f"inproj_c{l}")
        else:
            qc, kc, vc, bgc, vcuc = _inproj_call(ctx, g[0], cxm[1], cxm[0], w_in_l, True, f"inproj_c{l}")

        attn = _na_call(q, k, v, kc, vc, _bias_tables(rpb[l], s // GRID_W))
        moe = l % 2 == 1
        x, h, *route = _merge_call(attn, vcu, bg, x, w_out_l, conv_w[l], gout, g[1], lat[2], g[2],
                                   lat[4], lat[3], w_router[l // 2] if moe else None, f"merge_x{l}")
        if not last:
            attn_c = _ctx_attn_call(qc, kc, vc)
            ctx, hc = _merge_call(attn_c, vcuc, bgc, ctx, w_out_l, conv_w[l], gout, g[1], cxm[2], g[2],
                                  cxm[4], cxm[3], None, f"merge_c{l}")

        if moe:
            x = _moe_call(h, route[0], route[1], x, lat[5], g[3],
                          w_gu_moe[l // 2].astype(BF16), w_down_moe[l // 2].astype(BF16))
        else:
            wgu = w_gu_dense[l // 2].astype(BF16)
            wdn = w_down_dense[l // 2].astype(BF16)
            x = _ffn_call(h, x, wgu, wdn, g[3], lat[5], f"ffn_x{l}")
            if not last:
                ctx = _ffn_call(hc, ctx, wgu, wdn, g[3], cxm[5], f"ffn_c{l}")
    return x
```

```python
import functools

import numpy as np
import jax
import jax.numpy as jnp
from jax import lax
from jax.experimental import pallas as pl
from jax.experimental.pallas import tpu as pltpu

F32 = jnp.float32
BF16 = jnp.bfloat16
I32 = jnp.int32

EPS = 1e-6
GRID_W = 64
HEAD_DIM = 64
NA_KH = 8
NA_KW = 16
TOP_K = 2
LANES = 128
Q_ROWS = 4
BAND_ROWS = 12
ATTN_BLOCKS_PER_STEP = 4
DY_PAD = Q_ROWS
NEG = -1e30
HALO = 16
V7X_VMEM_CAP = 60 * 1024 * 1024
TOKEN_TILE = 512
ROUTE_TILE = 1024
FF_CHUNK = 512
ROW_DMA_UNROLL = 8
EXPERT_ROWS = 16


def _cparams(sem, vmem_bytes):
    return pltpu.CompilerParams(dimension_semantics=sem,
                                vmem_limit_bytes=int(min(vmem_bytes, V7X_VMEM_CAP)))


def _rms(x, g):
    return x * lax.rsqrt(jnp.mean(x * x, axis=-1, keepdims=True) + EPS) * g


def _dot(a, b):
    return jnp.dot(a, b, preferred_element_type=F32)


def _dot_nt(a, b):
    return lax.dot_general(a, b, (((1,), (1,)), ((), ())), preferred_element_type=F32)


def _split_bf16(a):
    hi = a.astype(BF16)
    lo = (a - hi.astype(F32)).astype(BF16)
    return hi, lo


def _dot3(a, w):
    ah, al = _split_bf16(a)
    wh, wl = _split_bf16(w)
    return _dot(ah, wh) + (_dot(ah, wl) + _dot(al, wh))


def _silu(x):
    return x / (1.0 + jnp.exp(-x))


def _ada_kernel(c_ref, w_ref, b_ref, o_ref):
    o_ref[0] = _dot3(_silu(c_ref[...]), w_ref[0]) + b_ref[0]


def _ada_call(cc, w_ada, b_ada):
    depth, d, n = w_ada.shape
    rows = cc.shape[0]
    tn = n // 4
    return pl.pallas_call(
        _ada_kernel,
        grid=(depth, n // tn),
        in_specs=[pl.BlockSpec((rows, d), lambda l, j: (0, 0)),
                  pl.BlockSpec((1, d, tn), lambda l, j: (l, 0, j)),
                  pl.BlockSpec((1, 1, tn), lambda l, j: (l, 0, j))],
        out_specs=pl.BlockSpec((1, rows, tn), lambda l, j: (l, 0, j)),
        out_shape=jax.ShapeDtypeStruct((depth, rows, n), F32),
        compiler_params=_cparams(("parallel", "parallel"), 6 * d * tn * 4),
        name="adaln",
    )(cc, w_ada, b_ada.reshape(depth, 1, n))


def _inproj_kernel(x_ref, g_ref, sc_ref, sh_ref, w_ref, *out_refs, full):
    h = _rms(x_ref[0], g_ref[...]) * (1.0 + sc_ref[0]) + sh_ref[0]
    hb = h.astype(BF16)
    cw = out_refs[0].shape[-1]

    def col(j):
        return _dot(hb, w_ref[:, j * cw:(j + 1) * cw])

    if full:
        q_ref, k_ref, v_ref, bg_ref, vcu_ref = out_refs
        q_ref[0] = col(0).astype(BF16)
        k_ref[0] = col(1).astype(BF16)
        v_ref[0] = col(2).astype(BF16)
        bg_ref[0] = col(3).astype(BF16)
        vcu_ref[0] = (col(4) * col(5)).astype(BF16)
    else:
        k_ref, v_ref = out_refs
        k_ref[0] = col(0).astype(BF16)
        v_ref[0] = col(1).astype(BF16)


def _inproj_call(x, g, sc, sh, w, full, name):
    b, s, d = x.shape
    cw = d // 2
    tm = min(TOKEN_TILE, s)
    n_out = 5 if full else 2
    per_batch = sc.shape[0] > 1
    mod_spec = pl.BlockSpec((1, 1, d), (lambda bi, i: (bi, 0, 0)) if per_batch else (lambda bi, i: (0, 0, 0)))
    out_spec = pl.BlockSpec((1, tm, cw), lambda bi, i: (bi, i, 0))
    vmem = 2 * tm * d * 4 + 2 * w.size * 2 + 2 * n_out * tm * cw * 2 + 8 * tm * d * 4
    return pl.pallas_call(
        functools.partial(_inproj_kernel, full=full),
        grid=(b, s // tm),
        in_specs=[pl.BlockSpec((1, tm, d), lambda bi, i: (bi, i, 0)),
                  pl.BlockSpec((1, d), lambda bi, i: (0, 0)),
                  mod_spec, mod_spec,
                  pl.BlockSpec(w.shape, lambda bi, i: (0, 0))],
        out_specs=[out_spec] * n_out,
        out_shape=[jax.ShapeDtypeStruct((b, s, cw), BF16)] * n_out,
        compiler_params=_cparams(("parallel", "parallel"), vmem),
        name=name,
    )(x, g, sc, sh, w)


def _head_pair_attention(q2, key_parts, value_parts, bias_parts):
    m_rows = q2.shape[0]
    lane = lax.broadcasted_iota(I32, q2.shape, 1)
    qs = q2 * (HEAD_DIM ** -0.5)
    qq = jnp.concatenate([jnp.where(lane < HEAD_DIM, qs, 0), jnp.where(lane >= HEAD_DIM, qs, 0)], axis=0)
    scores = []
    for kk, bias in zip(key_parts, bias_parts):
        s = _dot_nt(qq, kk)
        scores.append(s if bias is None else s + bias)
    m = scores[0].max(axis=-1, keepdims=True)
    for s in scores[1:]:
        m = jnp.maximum(m, s.max(axis=-1, keepdims=True))
    acc = None
    for s, vv in zip(scores, value_parts):
        v_aug = jnp.concatenate([vv, jnp.ones_like(vv)], axis=1)
        acc_i = _dot(jnp.exp(s - m).astype(BF16), v_aug)
        acc = acc_i if acc is None else acc + acc_i
    out = acc[:, :LANES] / acc[:, LANES:]
    return jnp.where(lane < HEAD_DIM, out[:m_rows], out[m_rows:])


def _na_kernel(q_ref, k_ref, v_ref, kc_ref, vc_ref, bias_ref, rmask_ref, o_ref, *, grid_rows, blocks_per_step):
    tq = Q_ROWS * GRID_W
    band = BAND_ROWS * GRID_W
    n_blocks = grid_rows // Q_ROWS
    n_pairs = q_ref.shape[-1] // LANES
    off_interior = NA_KH // 2 - 1 + DY_PAD
    off_first = NA_KH - 1 + DY_PAD
    off_last = Q_ROWS - BAND_ROWS + NA_KH - 1 + DY_PAD
    for jb in range(blocks_per_step):
        ib = pl.program_id(1) * blocks_per_step + jb
        start_row = jnp.clip(Q_ROWS * ib - NA_KH // 2, 0, grid_rows - BAND_ROWS)
        tok0 = pl.multiple_of(start_row * GRID_W, GRID_W)
        rows = slice(jb * tq, (jb + 1) * tq)
        may_be_edge = jb == 0 or jb == blocks_per_step - 1
        if may_be_edge:
            kind = jnp.where(ib == 0, 0, jnp.where(ib == n_blocks - 1, 2, 1))
            variant = jnp.where(kind == 1, 0, 1)
            off = jnp.where(kind == 1, off_interior, jnp.where(kind == 0, off_first, off_last))
        else:
            variant, off = 0, off_interior
        for p in range(n_pairs):
            ls = slice(p * LANES, (p + 1) * LANES)
            k2 = k_ref[0, pl.ds(tok0, band), ls]
            v2 = v_ref[0, pl.ds(tok0, band), ls]
            bias_rows = []
            for hh in range(2):
                for jq in range(Q_ROWS):
                    blocks = []
                    for ap in range(BAND_ROWS // 2):
                        blk = bias_ref[variant, 2 * p + hh, 2 * ap - jq + off]
                        blocks.append(blk + rmask_ref[kind, ap] if may_be_edge else blk)
                    bias_rows.append(jnp.concatenate(blocks, axis=1))
            o2 = _head_pair_attention(q_ref[0, rows, ls], [k2, kc_ref[0, :, ls]], [v2, vc_ref[0, :, ls]],
                                      [jnp.concatenate(bias_rows, axis=0), None])
            o_ref[0, rows, ls] = o2.astype(BF16)


def _na_call(q, k, v, kc, vc, bias, row_mask):
    b, s, cw = q.shape
    ctx = kc.shape[1]
    grid_rows = s // GRID_W
    tq = Q_ROWS * GRID_W
    nblk = s // tq
    bps = min(ATTN_BLOCKS_PER_STEP, nblk)
    band = BAND_ROWS * GRID_W
    vmem = (4 * s * cw * 2 + 4 * ctx * cw * 2 + bias.size * 4
            + 4 * bps * tq * cw * 2 + 16 * tq * (band + ctx) * 4)
    return pl.pallas_call(
        functools.partial(_na_kernel, grid_rows=grid_rows, blocks_per_step=bps),
        grid=(b, nblk // bps),
        in_specs=[pl.BlockSpec((1, bps * tq, cw), lambda bi, i: (bi, i, 0)),
                  pl.BlockSpec((1, s, cw), lambda bi, i: (bi, 0, 0)),
                  pl.BlockSpec((1, s, cw), lambda bi, i: (bi, 0, 0)),
                  pl.BlockSpec((1, ctx, cw), lambda bi, i: (bi, 0, 0)),
                  pl.BlockSpec((1, ctx, cw), lambda bi, i: (bi, 0, 0)),
                  pl.BlockSpec(bias.shape, lambda bi, i: (0, 0, 0, 0, 0)),
                  pl.BlockSpec(row_mask.shape, lambda bi, i: (0, 0, 0, 0))],
        out_specs=pl.BlockSpec((1, bps * tq, cw), lambda bi, i: (bi, i, 0)),
        out_shape=jax.ShapeDtypeStruct((b, s, cw), BF16),
        compiler_params=_cparams(("parallel", "arbitrary"), vmem),
        name="neigh_attn",
    )(q, k, v, kc, vc, bias, row_mask)


def _ctx_attn_kernel(q_ref, k_ref, v_ref, o_ref):
    n_pairs = q_ref.shape[-1] // LANES
    for p in range(n_pairs):
        ls = slice(p * LANES, (p + 1) * LANES)
        o2 = _head_pair_attention(q_ref[0, :, ls], [k_ref[0, :, ls]], [v_ref[0, :, ls]], [None])
        o_ref[0, :, ls] = o2.astype(BF16)


def _ctx_attn_call(q, k, v):
    b, s, cw = q.shape
    spec = pl.BlockSpec((1, s, cw), lambda bi: (bi, 0, 0))
    return pl.pallas_call(
        _ctx_attn_kernel,
        grid=(b,),
        in_specs=[spec, spec, spec],
        out_specs=spec,
        out_shape=jax.ShapeDtypeStruct((b, s, cw), BF16),
        compiler_params=_cparams(("parallel",), 8 * s * cw * 2 + 16 * s * s * 4),
        name="ctx_attn",
    )(q, k, v)


def _bias_tables(rpb_l):
    w = GRID_W
    n_heads, n_dy, n_dx = rpb_l.shape
    qc = np.arange(w)[:, None]
    kc = np.arange(w)[None, :]
    cs = np.clip(qc - NA_KW // 2, 0, w - NA_KW)
    ok_c = (kc >= cs) & (kc < cs + NA_KW)
    sel_x = ok_c[..., None] & ((kc - qc + NA_KW - 1)[..., None] == np.arange(n_dx))
    cols = jnp.einsum("hyx,wcx->hywc", rpb_l, jnp.asarray(sel_x, F32), precision=lax.Precision.HIGHEST)
    cols = jnp.where(jnp.asarray(ok_c), cols, NEG)
    dy = np.arange(n_dy)
    inner = (dy >= NA_KH // 2 - 1) & (dy < NA_KH // 2 - 1 + NA_KH)
    neg = jnp.full((n_heads, DY_PAD, w, w), NEG, F32)
    slots = jnp.stack([
        jnp.concatenate([neg, jnp.where(jnp.asarray(inner)[None, :, None, None], cols, NEG), neg], axis=1),
        jnp.concatenate([neg, cols, neg], axis=1)])
    blocks = jnp.concatenate([slots[:, :, :-1], slots[:, :, 1:]], axis=-1)
    a = np.arange(BAND_ROWS)
    ok_a = np.stack([a < NA_KH, a >= 0, a >= BAND_ROWS - NA_KH])
    row_mask = np.where(ok_a, 0.0, NEG).astype(np.float32).reshape(3, BAND_ROWS // 2, 2, 1)
    row_mask = np.broadcast_to(row_mask, (3, BAND_ROWS // 2, 2, w)).reshape(3, BAND_ROWS // 2, 1, 2 * w)
    return blocks, jnp.asarray(row_mask)


def _merge_kernel(attn_ref, vcu_ref, vprev_ref, vnext_ref, bg_ref, x_ref, wout_ref, convw_ref,
                  gout_ref, gpost_ref, gt_ref, gpre_ref, sc_ref, sh_ref, *rest, n_exp):
    moe = n_exp > 0
    i = pl.program_id(1)
    n = pl.num_programs(1)
    v = vcu_ref[0].astype(F32)
    tm, cw = v.shape
    row = lax.broadcasted_iota(I32, v.shape, 0)
    prev = jnp.where(i > 0, vprev_ref[0, HALO - 1:HALO, :].astype(F32), 0.0)
    nxt = jnp.where(i < n - 1, vnext_ref[0, 0:1, :].astype(F32), 0.0)
    v_m1 = jnp.where(row == 0, prev, pltpu.roll(v, 1, 0))
    v_p1 = jnp.where(row == tm - 1, nxt, pltpu.roll(v, tm - 1, 0))
    w3 = convw_ref[...]
    conv = w3[0:1] * v_m1 + w3[1:2] * v + w3[2:3] * v_p1
    cv = bg_ref[0].astype(F32) * conv
    g_out = gout_ref[...]
    a = _rms(attn_ref[0].astype(F32), g_out[:, :cw])
    c = _rms(cv, g_out[:, cw:])
    y = _dot(a.astype(BF16), wout_ref[:cw, :]) + _dot(c.astype(BF16), wout_ref[cw:, :])
    xn = x_ref[0] + gt_ref[0] * _rms(y, gpost_ref[...])
    h = _rms(xn, gpre_ref[...]) * (1.0 + sc_ref[0]) + sh_ref[0]
    if not moe:
        x_out_ref, h_ref = rest
        x_out_ref[0] = xn
        h_ref[0] = h.astype(BF16)
        return
    wr_ref, x_out_ref, h_ref, ti_ref, tw_ref = rest
    x_out_ref[0] = xn
    h_ref[0] = h
    hh, hl = _split_bf16(h)
    wh, wl = _split_bf16(wr_ref[...])
    both = _dot_nt(jnp.concatenate([wh, wl], axis=0), hh)
    logits = both[:EXPERT_ROWS] + (both[EXPERT_ROWS:] + _dot_nt(wh, hl))
    sub = lax.broadcasted_iota(I32, logits.shape, 0).astype(F32)
    logits = jnp.where(sub < n_exp, logits, NEG)
    big = float(EXPERT_ROWS)
    m1 = logits.max(axis=0, keepdims=True)
    i1 = jnp.where(logits == m1, sub, big).min(axis=0, keepdims=True)
    l2 = jnp.where(sub == i1, NEG, logits)
    m2 = l2.max(axis=0, keepdims=True)
    i2 = jnp.where(l2 == m2, sub, big).min(axis=0, keepdims=True)
    e = jnp.exp(m2 - m1)
    w1 = 1.0 / (1.0 + e)
    ti_ref[0] = jnp.concatenate([i1, i2], axis=0).astype(I32)
    tw_ref[0] = jnp.concatenate([w1, e * w1], axis=0)


def _merge_call(attn, vcu, bg, x, wout, convw, gout, gpost, gt, gpre, sc, sh, w_router, name):
    b, s, d = x.shape
    cw = attn.shape[-1]
    tm = min(TOKEN_TILE, s)
    nt = s // tm
    moe = w_router is not None
    n_exp = w_router.shape[-1] if moe else 0
    if moe:
        wr = jnp.zeros((EXPERT_ROWS, d), F32).at[:n_exp].set(w_router.T)
    per_batch = gt.shape[0] > 1
    hb = tm // HALO
    n_halo = s // HALO
    tile = lambda width: pl.BlockSpec((1, tm, width), lambda bi, i: (bi, i, 0))
    vec = lambda width: pl.BlockSpec((1, width), lambda bi, i: (0, 0))
    mod = pl.BlockSpec((1, 1, d), (lambda bi, i: (bi, 0, 0)) if per_batch else (lambda bi, i: (0, 0, 0)))
    in_specs = [tile(cw), tile(cw),
                pl.BlockSpec((1, HALO, cw), lambda bi, i: (bi, jnp.maximum(i * hb - 1, 0), 0)),
                pl.BlockSpec((1, HALO, cw), lambda bi, i: (bi, jnp.minimum((i + 1) * hb, n_halo - 1), 0)),
                tile(cw), tile(d),
                pl.BlockSpec(wout.shape, lambda bi, i: (0, 0)),
                pl.BlockSpec(convw.shape, lambda bi, i: (0, 0)),
                vec(d), vec(d), mod, vec(d), mod, mod]
    args = [attn, vcu, vcu, vcu, bg, x, wout, convw, gout, gpost, gt, gpre, sc, sh]
    out_specs = [tile(d), tile(d)]
    out_shape = [jax.ShapeDtypeStruct((b, s, d), F32),
                 jax.ShapeDtypeStruct((b, s, d), F32 if moe else BF16)]
    if moe:
        in_specs.append(pl.BlockSpec(wr.shape, lambda bi, i: (0, 0)))
        args.append(wr)
        route = pl.BlockSpec((1, TOP_K, tm), lambda bi, i: (bi, 0, i))
        out_specs += [route, route]
        out_shape += [jax.ShapeDtypeStruct((b, TOP_K, s), I32), jax.ShapeDtypeStruct((b, TOP_K, s), F32)]
    vmem = 2 * wout.size * 2 + 2 * tm * (3 * cw * 2 + 3 * d * 4) + 16 * tm * d * 4
    return pl.pallas_call(
        functools.partial(_merge_kernel, n_exp=n_exp),
        grid=(b, nt),
        in_specs=in_specs,
        out_specs=out_specs,
        out_shape=out_shape,
        compiler_params=_cparams(("parallel", "arbitrary"), vmem),
        name=name,
    )(*args)


def _swiglu(hb, wgu, wdn, ff):
    chunk = min(FF_CHUNK, ff)
    acc = None
    for c in range(ff // chunk):
        lo, hi = c * chunk, (c + 1) * chunk
        g = _dot(hb, wgu(lo, hi))
        u = _dot(hb, wgu(ff + lo, ff + hi))
        part = _dot((_silu(g) * u).astype(BF16), wdn(lo, hi))
        acc = part if acc is None else acc + part
    return acc


def _ffn_kernel(h_ref, x_ref, wgu_ref, wdn_ref, gpost_ref, gt_ref, o_ref):
    ff = wdn_ref.shape[0]
    y = _swiglu(h_ref[0], lambda lo, hi: wgu_ref[:, lo:hi], lambda lo, hi: wdn_ref[lo:hi, :], ff)
    o_ref[0] = x_ref[0] + gt_ref[0] * _rms(y, gpost_ref[...])


def _ffn_call(h, x, wgu, wdn, gpost, gt, name):
    b, s, d = x.shape
    tm = min(TOKEN_TILE, s)
    per_batch = gt.shape[0] > 1
    tile = pl.BlockSpec((1, tm, d), lambda bi, i: (bi, i, 0))
    vmem = (wgu.size + wdn.size) * 2 + 2 * tm * d * (2 + 4 + 4) + 10 * tm * d * 4
    return pl.pallas_call(
        _ffn_kernel,
        grid=(b, s // tm),
        in_specs=[tile, tile,
                  pl.BlockSpec(wgu.shape, lambda bi, i: (0, 0)),
                  pl.BlockSpec(wdn.shape, lambda bi, i: (0, 0)),
                  pl.BlockSpec((1, d), lambda bi, i: (0, 0)),
                  pl.BlockSpec((1, 1, d), (lambda bi, i: (bi, 0, 0)) if per_batch else (lambda bi, i: (0, 0, 0)))],
        out_specs=tile,
        out_shape=jax.ShapeDtypeStruct((b, s, d), F32),
        compiler_params=_cparams(("parallel", "parallel"), vmem),
        name=name,
    )(h, x, wgu, wdn, gpost, gt)


def _plan_kernel(ti_ref, rank_ref, cnt_ref, carry):
    @pl.when(pl.program_id(0) == 0)
    def _():
        carry[...] = jnp.zeros_like(carry)

    ti = ti_ref[...]
    tm = ti.shape[1]
    sub = lax.broadcasted_iota(I32, (EXPERT_ROWS, tm), 0)
    e1 = sub == ti[0:1, :]
    e2 = sub == ti[1:2, :]
    c = e1.astype(F32) + e2.astype(F32)
    tri = (lax.broadcasted_iota(I32, (tm, tm), 0) < lax.broadcasted_iota(I32, (tm, tm), 1)).astype(BF16)
    before = _dot(c.astype(BF16), tri) + carry[:, 0:1]
    r1 = jnp.where(e1, before, 0.0).sum(axis=0, keepdims=True)
    r2 = jnp.where(e2, before, 0.0).sum(axis=0, keepdims=True)
    rank_ref[...] = jnp.concatenate([r1, r2], axis=0).astype(I32)
    carry[...] += c.sum(axis=1, keepdims=True)
    cnt_ref[...] = carry[...]


def _plan_call(ti):
    n = ti.shape[1]
    tm = min(TOKEN_TILE, n)
    return pl.pallas_call(
        _plan_kernel,
        grid=(n // tm,),
        in_specs=[pl.BlockSpec((TOP_K, tm), lambda j: (0, j))],
        out_specs=[pl.BlockSpec((TOP_K, tm), lambda j: (0, j)),
                   pl.BlockSpec((EXPERT_ROWS, LANES), lambda j: (0, 0))],
        out_shape=[jax.ShapeDtypeStruct((TOP_K, n), I32), jax.ShapeDtypeStruct((EXPERT_ROWS, LANES), F32)],
        scratch_shapes=[pltpu.VMEM((EXPERT_ROWS, LANES), F32)],
        compiler_params=_cparams(("arbitrary",), 8 * tm * tm * 4 + 16 * tm * LANES * 4),
        name="route_plan",
    )(ti)


def _row_copy(src, src_row, dst, dst_row, sem):
    return pltpu.make_async_copy(src.at[pl.ds(src_row, 1), :], dst.at[pl.ds(dst_row, 1), :], sem)


def _dispatch_kernel(p1_ref, p2_ref, pad_ref, h_ref, o_ref, zrow, sem):
    tm = h_ref.shape[0]
    j = pl.program_id(0)
    base = j * tm

    @pl.when(j == 0)
    def _():
        zrow[...] = jnp.zeros_like(zrow)
        for e in range(pad_ref.shape[0] // 2):
            lo, hi = pad_ref[2 * e], pad_ref[2 * e + 1]
            lax.fori_loop(lo, hi, lambda r, c: (_row_copy(zrow, 0, o_ref, r, sem).start(), c)[1], 0)
            lax.fori_loop(lo, hi, lambda r, c: (_row_copy(zrow, 0, o_ref, 0, sem).wait(), c)[1], 0)

    def issue(t, carry):
        _row_copy(h_ref, t, o_ref, p1_ref[base + t], sem).start(priority=0)
        _row_copy(h_ref, t, o_ref, p2_ref[base + t], sem).start(priority=1)
        return carry

    def drain(t, carry):
        _row_copy(h_ref, 0, o_ref, 0, sem).wait()
        _row_copy(h_ref, 0, o_ref, 0, sem).wait()
        return carry

    lax.fori_loop(0, tm, issue, 0, unroll=ROW_DMA_UNROLL)
    lax.fori_loop(0, tm, drain, 0, unroll=ROW_DMA_UNROLL)


def _dispatch_call(pos1, pos2, pad_ranges, h, n_rows):
    n, d = h.shape
    tm = min(ROUTE_TILE, n)
    return pl.pallas_call(
        _dispatch_kernel,
        grid_spec=pltpu.PrefetchScalarGridSpec(
            num_scalar_prefetch=3,
            grid=(n // tm,),
            in_specs=[pl.BlockSpec((tm, d), lambda j, p1, p2, pad: (j, 0))],
            out_specs=pl.BlockSpec(memory_space=pl.ANY),
            scratch_shapes=[pltpu.VMEM((8, d), F32), pltpu.SemaphoreType.DMA(())]),
        out_shape=jax.ShapeDtypeStruct((n_rows, d), F32),
        compiler_params=_cparams(("arbitrary",), 6 * tm * d * 4),
        name="route_dispatch",
    )(pos1, pos2, pad_ranges, h)


def _experts_kernel(te_ref, tv_ref, h_ref, wgu_ref, wdn_ref, y_ref):
    j = pl.program_id(0)
    ff = wdn_ref.shape[1]

    @pl.when(tv_ref[j] > 0)
    def _():
        y_ref[...] = _swiglu(h_ref[...].astype(BF16),
                             lambda lo, hi: wgu_ref[0, :, lo:hi],
                             lambda lo, hi: wdn_ref[0, lo:hi, :], ff)

    @pl.when(tv_ref[j] == 0)
    def _():
        y_ref[...] = jnp.zeros_like(y_ref)


def _experts_call(tile_expert, tile_valid, h_sorted, wgu, wdn, tm):
    n_rows, d = h_sorted.shape
    n_exp, _, ff2 = wgu.shape
    ff = wdn.shape[1]
    vmem = (wgu.size + wdn.size) // n_exp * 2 + 4 * tm * d * 4 + 10 * tm * d * 4
    return pl.pallas_call(
        _experts_kernel,
        grid_spec=pltpu.PrefetchScalarGridSpec(
            num_scalar_prefetch=2,
            grid=(n_rows // tm,),
            in_specs=[pl.BlockSpec((tm, d), lambda j, te, tv: (j, 0)),
                      pl.BlockSpec((1, d, ff2), lambda j, te, tv: (te[j], 0, 0),
                                   pipeline_mode=pl.Buffered(1)),
                      pl.BlockSpec((1, ff, d), lambda j, te, tv: (te[j], 0, 0),
                                   pipeline_mode=pl.Buffered(1))],
            out_specs=pl.BlockSpec((tm, d), lambda j, te, tv: (j, 0))),
        out_shape=jax.ShapeDtypeStruct((n_rows, d), F32),
        compiler_params=_cparams(("arbitrary",), vmem),
        name="route_experts",
    )(tile_expert, tile_valid, h_sorted, wgu, wdn)


def _combine_kernel(p1_ref, p2_ref, tw_ref, x_ref, gt_ref, gpost_ref, y_ref, o_ref, buf, sem):
    tm = x_ref.shape[0]
    j = pl.program_id(0)
    last = pl.num_programs(0) - 1
    slot = j % 2

    def gather_tile(tile, s):
        base = tile * tm

        def issue(t, carry):
            _row_copy(y_ref, p1_ref[base + t], buf.at[s, 0], t, sem.at[s]).start(priority=0)
            _row_copy(y_ref, p2_ref[base + t], buf.at[s, 1], t, sem.at[s]).start(priority=1)
            return carry

        lax.fori_loop(0, tm, issue, 0, unroll=ROW_DMA_UNROLL)

    @pl.when(j == 0)
    def _():
        gather_tile(0, 0)

    @pl.when(j < last)
    def _():
        gather_tile(j + 1, 1 - slot)

    def drain(t, carry):
        _row_copy(y_ref, 0, buf.at[slot, 0], 0, sem.at[slot]).wait()
        _row_copy(y_ref, 0, buf.at[slot, 1], 0, sem.at[slot]).wait()
        return carry

    lax.fori_loop(0, tm, drain, 0, unroll=ROW_DMA_UNROLL)
    tw = tw_ref[...]
    y = tw[:, 0:1] * buf[slot, 0] + tw[:, 1:2] * buf[slot, 1]
    o_ref[...] = x_ref[...] + gt_ref[0] * _rms(y, gpost_ref[...])


def _combine_call(pos1, pos2, tw, x, gt, gpost, y_sorted, seq):
    n, d = x.shape
    tm = min(ROUTE_TILE, seq)
    per_seq = seq // tm
    return pl.pallas_call(
        _combine_kernel,
        grid_spec=pltpu.PrefetchScalarGridSpec(
            num_scalar_prefetch=2,
            grid=(n // tm,),
            in_specs=[pl.BlockSpec((tm, TOP_K), lambda j, p1, p2: (j, 0)),
                      pl.BlockSpec((tm, d), lambda j, p1, p2: (j, 0)),
                      pl.BlockSpec((1, 1, d), lambda j, p1, p2: (j // per_seq, 0, 0)),
                      pl.BlockSpec((1, d), lambda j, p1, p2: (0, 0)),
                      pl.BlockSpec(memory_space=pl.ANY)],
            out_specs=pl.BlockSpec((tm, d), lambda j, p1, p2: (j, 0)),
            scratch_shapes=[pltpu.VMEM((2, TOP_K, tm, d), F32), pltpu.SemaphoreType.DMA((2,))]),
        out_shape=jax.ShapeDtypeStruct((n, d), F32),
        compiler_params=_cparams(("arbitrary",), 16 * tm * d * 4),
        name="route_combine",
    )(pos1, pos2, tw, x, gt, gpost, y_sorted)


def _moe_call(h, ti, tw, x, gt, gpost, wgu, wdn):
    b, s, d = x.shape
    n = b * s
    n_exp = wgu.shape[0]
    tm = TOKEN_TILE
    ti = ti.transpose(1, 0, 2).reshape(TOP_K, n)
    rank, cnt = _plan_call(ti)
    counts = cnt[:n_exp, 0].astype(I32)
    padded = (counts + tm - 1) // tm * tm
    ends = jnp.cumsum(padded)
    starts = ends - padded
    experts = jnp.arange(n_exp, dtype=I32)
    pos = rank + jnp.sum(jnp.where(ti[..., None] == experts, starts, 0), axis=-1)
    n_tiles = n * TOP_K // tm + n_exp
    n_rows = n_tiles * tm
    row0 = jnp.arange(n_tiles, dtype=I32) * tm
    tile_expert = jnp.minimum(jnp.sum(row0[:, None] >= ends[None, :], axis=1), n_exp - 1).astype(I32)
    tile_valid = (row0 < ends[-1]).astype(I32)
    pad_lo = jnp.concatenate([starts + counts, ends[-1:]])
    pad_hi = jnp.concatenate([ends, jnp.full((1,), n_rows, I32)])
    pad_ranges = jnp.stack([pad_lo, pad_hi], axis=1).reshape(-1).astype(I32)
    pos1, pos2 = pos[0], pos[1]
    h_sorted = _dispatch_call(pos1, pos2, pad_ranges, h.reshape(n, d), n_rows)
    y_sorted = _experts_call(tile_expert, tile_valid, h_sorted, wgu, wdn, tm)
    tw = tw.transpose(0, 2, 1).reshape(n, TOP_K)
    out = _combine_call(pos1, pos2, tw, x.reshape(n, d), gt, gpost, y_sorted, s)
    return out.reshape(b, s, d)


def kernel(x, c, ctx, c_ctx, w_ada, b_ada, norm_g, w_in, rpb, conv_w, out_norm_g, w_out,
           w_gu_dense, w_down_dense, w_router, w_gu_moe, w_down_moe):
    depth = w_ada.shape[0]
    b, s, d = x.shape
    cw = d // 2

    cc =jnp.zeros((16, d), F32).at[:b].set(c).at[b].set(c_ctx)
    mods = _ada_call(cc, w_ada, b_ada)

    for l in range(depth):
        last = l == depth - 1
        m = mods[l].reshape(16, 6, d)
        lat = [m[:b, j][:, None, :] for j in range(6)]
        cxm = [m[b:b + 1, j][:, None, :] for j in range(6)]
        g = [norm_g[l, j][None, :] for j in range(4)]
        w_in_l = w_in[l].astype(BF16)
        w_out_l = w_out[l].astype(BF16)
        gout = out_norm_g[l][None, :]

        q, k, v, bg, vcu = _inproj_call(x, g[0], lat[1], lat[0], w_in_l, True, f"inproj_x{l}")
        if last:
            kc, vc = _inproj_call(ctx, g[0], cxm[1], cxm[0], w_in_l[:, cw:3 * cw], False, f"inproj_c{l}")
        else:
            qc, kc, vc, bgc, vcuc = _inproj_call(ctx, g[0], cxm[1], cxm[0], w_in_l, True, f"inproj_c{l}")

        attn = _na_call(q, k, v, kc, vc, *_bias_tables(rpb[l]))
        moe = l % 2 == 1
        x, h, *route = _merge_call(attn, vcu, bg, x, w_out_l, conv_w[l], gout, g[1], lat[2], g[2],
                                   lat[4], lat[3], w_router[l // 2] if moe else None, f"merge_x{l}")
        if not last:
            attn_c = _ctx_attn_call(qc, kc, vc)
            ctx, hc = _merge_call(attn_c, vcuc, bgc, ctx, w_out_l, conv_w[l], gout, g[1], cxm[2], g[2],
                                  cxm[4], cxm[3], None, f"merge_c{l}")

        if moe:
            x = _moe_call(h, route[0], route[1], x, lat[5], g[3],
                          w_gu_moe[l // 2].astype(BF16), w_down_moe[l // 2].astype(BF16))
        else:
            wgu = w_gu_dense[l // 2].astype(BF16)
            wdn = w_down_dense[l // 2].astype(BF16)
            x = _ffn_call(h, x, wgu, wdn, g[3], lat[5], f"ffn_x{l}")
            if not last:
                ctx = _ffn_call(hc, ctx, wgu, wdn, g[3], cxm[5], f"ffn_c{l}")
    return x
```

```python
import functools

import numpy as np
import jax
import jax.numpy as jnp
from jax import lax
from jax.experimental import pallas as pl
from jax.experimental.pallas import tpu as pltpu

F32 = jnp.float32
BF16 = jnp.bfloat16
I32 = jnp.int32

EPS = 1e-6
GRID_W = 64
HEAD_DIM = 64
NA_KH = 8
NA_KW = 16
TOP_K = 2
LANES = 128
Q_ROWS = 4
BAND_ROWS = 12
ATTN_BLOCKS_PER_STEP = 4
DY_PAD = Q_ROWS
NEG = -1e30
HALO = 16
V7X_VMEM_CAP = 60 * 1024 * 1024
TOKEN_TILE = 512
ROUTE_TILE = 1024
FF_CHUNK = 512
ROW_DMA_UNROLL = 8
EXPERT_ROWS = 16


def _cparams(sem, vmem_bytes):
    return pltpu.CompilerParams(dimension_semantics=sem,
                                vmem_limit_bytes=int(min(vmem_bytes, V7X_VMEM_CAP)))


def _rms(x, g):
    return x * lax.rsqrt(jnp.mean(x * x, axis=-1, keepdims=True) + EPS) * g


def _dot(a, b):
    return jnp.dot(a, b, preferred_element_type=F32)


def _dot_nt(a, b):
    return lax.dot_general(a, b, (((1,), (1,)), ((), ())), preferred_element_type=F32)


def _split_bf16(a):
    hi = a.astype(BF16)
    lo = (a - hi.astype(F32)).astype(BF16)
    return hi, lo


def _dot3(a, w):
    ah, al = _split_bf16(a)
    wh, wl = _split_bf16(w)
    return _dot(ah, wh) + (_dot(ah, wl) + _dot(al, wh))


def _silu(x):
    return x / (1.0 + jnp.exp(-x))


def _to_token_tiles(ref, val):
    rows, d = val.shape
    r = d // LANES
    for c in range(r):
        ref[pl.ds(c, rows, stride=r), :] = val[:, c * LANES:(c + 1) * LANES]


def _from_token_tiles(ref, rows):
    r = ref.shape[0] // rows
    return jnp.concatenate([ref[pl.ds(c, rows, stride=r), :] for c in range(r)], axis=1)


def _token_copy(src, src_tok, dst, dst_tok, r, sem):
    s0 = pl.multiple_of(src_tok * r, r)
    d0 = pl.multiple_of(dst_tok * r, r)
    return pltpu.make_async_copy(src.at[pl.ds(s0, r), :], dst.at[pl.ds(d0, r), :], sem)


def _ada_kernel(c_ref, w_ref, b_ref, o_ref):
    o_ref[0] = _dot3(_silu(c_ref[...]), w_ref[0]) + b_ref[0]


def _ada_call(cc, w_ada, b_ada):
    depth, d, n = w_ada.shape
    rows = cc.shape[0]
    tn = n // 4
    return pl.pallas_call(
        _ada_kernel,
        grid=(depth, n // tn),
        in_specs=[pl.BlockSpec((rows, d), lambda l, j: (0, 0)),
                  pl.BlockSpec((1, d, tn), lambda l, j: (l, 0, j)),
                  pl.BlockSpec((1, 1, tn), lambda l, j: (l, 0, j))],
        out_specs=pl.BlockSpec((1, rows, tn), lambda l, j: (l, 0, j)),
        out_shape=jax.ShapeDtypeStruct((depth, rows, n), F32),
        compiler_params=_cparams(("parallel", "parallel"), 6 * d * tn * 4),
        name="adaln",
    )(cc, w_ada, b_ada.reshape(depth, 1, n))


def _inproj_kernel(x_ref, g_ref, sc_ref, sh_ref, w_ref, *out_refs, full):
    h = _rms(x_ref[0], g_ref[...]) * (1.0 + sc_ref[0]) + sh_ref[0]
    hb = h.astype(BF16)
    cw = out_refs[0].shape[-1]

    def col(j):
        return _dot(hb, w_ref[:, j * cw:(j + 1) * cw])

    if full:
        q_ref, k_ref, v_ref, bg_ref, vcu_ref = out_refs
        q_ref[0] = col(0).astype(BF16)
        k_ref[0] = col(1).astype(BF16)
        v_ref[0] = col(2).astype(BF16)
        bg_ref[0] = col(3).astype(BF16)
        vcu_ref[0] = (col(4) * col(5)).astype(BF16)
    else:
        k_ref, v_ref = out_refs
        k_ref[0] = col(0).astype(BF16)
        v_ref[0] = col(1).astype(BF16)


def _inproj_call(x, g, sc, sh, w, full, name):
    b, s, d = x.shape
    cw = d // 2
    tm = min(TOKEN_TILE, s)
    n_out = 5 if full else 2
    per_batch = sc.shape[0] > 1
    mod_spec = pl.BlockSpec((1, 1, d), (lambda bi, i: (bi, 0, 0)) if per_batch else (lambda bi, i: (0, 0, 0)))
    out_spec = pl.BlockSpec((1, tm, cw), lambda bi, i: (bi, i, 0))
    vmem = 2 * tm * d * 4 + 2 * w.size * 2 + 2 * n_out * tm * cw * 2 + 8 * tm * d * 4
    return pl.pallas_call(
        functools.partial(_inproj_kernel, full=full),
        grid=(b, s // tm),
        in_specs=[pl.BlockSpec((1, tm, d), lambda bi, i: (bi, i, 0)),
                  pl.BlockSpec((1, d), lambda bi, i: (0, 0)),
                  mod_spec, mod_spec,
                  pl.BlockSpec(w.shape, lambda bi, i: (0, 0))],
        out_specs=[out_spec] * n_out,
        out_shape=[jax.ShapeDtypeStruct((b, s, cw), BF16)] * n_out,
        compiler_params=_cparams(("parallel", "parallel"), vmem),
        name=name,
    )(x, g, sc, sh, w)


def _head_pair_attention(q2, key_parts, value_parts, bias_parts):
    m_rows = q2.shape[0]
    lane = lax.broadcasted_iota(I32, q2.shape, 1)
    qs = q2 * (HEAD_DIM ** -0.5)
    qq = jnp.concatenate([jnp.where(lane < HEAD_DIM, qs, 0), jnp.where(lane >= HEAD_DIM, qs, 0)], axis=0)
    scores = []
    for kk, bias in zip(key_parts, bias_parts):
        s = _dot_nt(qq, kk)
        scores.append(s if bias is None else s + bias)
    m = scores[0].max(axis=-1, keepdims=True)
    for s in scores[1:]:
        m = jnp.maximum(m, s.max(axis=-1, keepdims=True))
    acc = None
    for s, vv in zip(scores, value_parts):
        v_aug = jnp.concatenate([vv, jnp.ones_like(vv)], axis=1)
        acc_i = _dot(jnp.exp(s - m).astype(BF16), v_aug)
        acc = acc_i if acc is None else acc + acc_i
    out = acc[:, :LANES] / acc[:, LANES:]
    return jnp.where(lane < HEAD_DIM, out[:m_rows], out[m_rows:])


def _na_kernel(q_ref, k_ref, v_ref, kc_ref, vc_ref, bias_ref, rmask_ref, o_ref, *, grid_rows, blocks_per_step):
    tq = Q_ROWS * GRID_W
    band = BAND_ROWS * GRID_W
    n_blocks = grid_rows // Q_ROWS
    n_pairs = q_ref.shape[-1] // LANES
    off_interior = NA_KH // 2 - 1 + DY_PAD
    off_first = NA_KH - 1 + DY_PAD
    off_last = Q_ROWS - BAND_ROWS + NA_KH - 1 + DY_PAD
    for jb in range(blocks_per_step):
        ib = pl.program_id(1) * blocks_per_step + jb
        start_row = jnp.clip(Q_ROWS * ib - NA_KH // 2, 0, grid_rows - BAND_ROWS)
        tok0 = pl.multiple_of(start_row * GRID_W, GRID_W)
        rows = slice(jb * tq, (jb + 1) * tq)
        may_be_edge = jb == 0 or jb == blocks_per_step - 1
        if may_be_edge:
            kind = jnp.where(ib == 0, 0, jnp.where(ib == n_blocks - 1, 2, 1))
            variant = jnp.where(kind == 1, 0, 1)
            off = jnp.where(kind == 1, off_interior, jnp.where(kind == 0, off_first, off_last))
        else:
            variant, off = 0, off_interior
        for p in range(n_pairs):
            ls = slice(p * LANES, (p + 1) * LANES)
            k2 = k_ref[0, pl.ds(tok0, band), ls]
            v2 = v_ref[0, pl.ds(tok0, band), ls]
            bias_rows = []
            for hh in range(2):
                for jq in range(Q_ROWS):
                    blocks = []
                    for ap in range(BAND_ROWS // 2):
                        blk = bias_ref[variant, 2 * p + hh, 2 * ap - jq + off]
                        blocks.append(blk + rmask_ref[kind, ap] if may_be_edge else blk)
                    bias_rows.append(jnp.concatenate(blocks, axis=1))
            o2 = _head_pair_attention(q_ref[0, rows, ls], [k2, kc_ref[0, :, ls]], [v2, vc_ref[0, :, ls]],
                                      [jnp.concatenate(bias_rows, axis=0), None])
            o_ref[0, rows, ls] = o2.astype(BF16)


def _na_call(q, k, v, kc, vc, bias, row_mask):
    b, s, cw = q.shape
    ctx = kc.shape[1]
    grid_rows = s // GRID_W
    tq = Q_ROWS * GRID_W
    nblk = s // tq
    bps = min(ATTN_BLOCKS_PER_STEP, nblk)
    band = BAND_ROWS * GRID_W
    vmem = (4 * s * cw * 2 + 4 * ctx * cw * 2 + bias.size * 4
            + 4 * bps * tq * cw * 2 + 16 * tq * (band + ctx) * 4)
    return pl.pallas_call(
        functools.partial(_na_kernel, grid_rows=grid_rows, blocks_per_step=bps),
        grid=(b, nblk // bps),
        in_specs=[pl.BlockSpec((1, bps * tq, cw), lambda bi, i: (bi, i, 0)),
                  pl.BlockSpec((1, s, cw), lambda bi, i: (bi, 0, 0)),
                  pl.BlockSpec((1, s, cw), lambda bi, i: (bi, 0, 0)),
                  pl.BlockSpec((1, ctx, cw), lambda bi, i: (bi, 0, 0)),
                  pl.BlockSpec((1, ctx, cw), lambda bi, i: (bi, 0, 0)),
                  pl.BlockSpec(bias.shape, lambda bi, i: (0, 0, 0, 0, 0)),
                  pl.BlockSpec(row_mask.shape, lambda bi, i: (0, 0, 0, 0))],
        out_specs=pl.BlockSpec((1, bps * tq, cw), lambda bi, i: (bi, i, 0)),
        out_shape=jax.ShapeDtypeStruct((b, s, cw), BF16),
        compiler_params=_cparams(("parallel", "arbitrary"), vmem),
        name="neigh_attn",
    )(q, k, v, kc, vc, bias, row_mask)


def _ctx_attn_kernel(q_ref, k_ref, v_ref, o_ref):
    n_pairs = q_ref.shape[-1] // LANES
    for p in range(n_pairs):
        ls = slice(p * LANES, (p + 1) * LANES)
        o2 = _head_pair_attention(q_ref[0, :, ls], [k_ref[0, :, ls]], [v_ref[0, :, ls]], [None])
        o_ref[0, :, ls] = o2.astype(BF16)


def _ctx_attn_call(q, k, v):
    b, s, cw = q.shape
    spec = pl.BlockSpec((1, s, cw), lambda bi: (bi, 0, 0))
    return pl.pallas_call(
        _ctx_attn_kernel,
        grid=(b,),
        in_specs=[spec, spec, spec],
        out_specs=spec,
        out_shape=jax.ShapeDtypeStruct((b, s, cw), BF16),
        compiler_params=_cparams(("parallel",), 8 * s * cw * 2 + 16 * s * s * 4),
        name="ctx_attn",
    )(q, k, v)


def _bias_tables(rpb_l):
    w = GRID_W
    n_heads, n_dy, n_dx = rpb_l.shape
    qc = np.arange(w)[:, None]
    kc = np.arange(w)[None, :]
    cs = np.clip(qc - NA_KW // 2, 0, w - NA_KW)
    ok_c = (kc >= cs) & (kc < cs + NA_KW)
    sel_x = ok_c[..., None] & ((kc - qc + NA_KW - 1)[..., None] == np.arange(n_dx))
    cols = jnp.einsum("hyx,wcx->hywc", rpb_l, jnp.asarray(sel_x, F32), precision=lax.Precision.HIGHEST)
    cols = jnp.where(jnp.asarray(ok_c), cols, NEG)
    dy = np.arange(n_dy)
    inner = (dy >= NA_KH // 2 - 1) & (dy < NA_KH // 2 - 1 + NA_KH)
    neg = jnp.full((n_heads, DY_PAD, w, w), NEG, F32)
    slots = jnp.stack([
        jnp.concatenate([neg, jnp.where(jnp.asarray(inner)[None, :, None, None], cols, NEG), neg], axis=1),
        jnp.concatenate([neg, cols, neg], axis=1)])
    blocks = jnp.concatenate([slots[:, :, :-1], slots[:, :, 1:]], axis=-1)
    a = np.arange(BAND_ROWS)
    ok_a = np.stack([a < NA_KH, a >= 0, a >= BAND_ROWS - NA_KH])
    row_mask = np.where(ok_a, 0.0, NEG).astype(np.float32).reshape(3, BAND_ROWS // 2, 2, 1)
    row_mask = np.broadcast_to(row_mask, (3, BAND_ROWS // 2, 2, w)).reshape(3, BAND_ROWS // 2, 1, 2 * w)
    return blocks, jnp.asarray(row_mask)


def _merge_kernel(attn_ref, vcu_ref, vprev_ref, vnext_ref, bg_ref, x_ref, wout_ref, convw_ref,
                  gout_ref, gpost_ref, gt_ref, gpre_ref, sc_ref, sh_ref, *rest, n_exp):
    moe = n_exp > 0
    i = pl.program_id(1)
    n = pl.num_programs(1)
    v = vcu_ref[0].astype(F32)
    tm, cw = v.shape
    row = lax.broadcasted_iota(I32, v.shape, 0)
    prev = jnp.where(i > 0, vprev_ref[0, HALO - 1:HALO, :].astype(F32), 0.0)
    nxt = jnp.where(i < n - 1, vnext_ref[0, 0:1, :].astype(F32), 0.0)
    v_m1 = jnp.where(row == 0, prev, pltpu.roll(v, 1, 0))
    v_p1 = jnp.where(row == tm - 1, nxt, pltpu.roll(v, tm - 1, 0))
    w3 = convw_ref[...]
    conv = w3[0:1] * v_m1 + w3[1:2] * v + w3[2:3] * v_p1
    cv = bg_ref[0].astype(F32) * conv
    g_out = gout_ref[...]
    a = _rms(attn_ref[0].astype(F32), g_out[:, :cw])
    c = _rms(cv, g_out[:, cw:])
    y = _dot(a.astype(BF16), wout_ref[:cw, :]) + _dot(c.astype(BF16), wout_ref[cw:, :])
    xn = x_ref[0] + gt_ref[0] * _rms(y, gpost_ref[...])
    h = _rms(xn, gpre_ref[...]) * (1.0 + sc_ref[0]) + sh_ref[0]
    if not moe:
        x_out_ref, h_ref = rest
        x_out_ref[0] = xn
        h_ref[0] = h.astype(BF16)
        return
    wr_ref, x_out_ref, h_ref, ti_ref, tw_ref = rest
    x_out_ref[0] = xn
    _to_token_tiles(h_ref.at[0], h)
    hh, hl = _split_bf16(h)
    wh, wl = _split_bf16(wr_ref[...])
    both = _dot_nt(jnp.concatenate([wh, wl], axis=0), hh)
    logits = both[:EXPERT_ROWS] + (both[EXPERT_ROWS:] + _dot_nt(wh, hl))
    sub = lax.broadcasted_iota(I32, logits.shape, 0).astype(F32)
    logits = jnp.where(sub < n_exp, logits, NEG)
    big = float(EXPERT_ROWS)
    m1 = logits.max(axis=0, keepdims=True)
    i1 = jnp.where(logits == m1, sub, big).min(axis=0, keepdims=True)
    l2 = jnp.where(sub == i1, NEG, logits)
    m2 = l2.max(axis=0, keepdims=True)
    i2 = jnp.where(l2 == m2, sub, big).min(axis=0, keepdims=True)
    e = jnp.exp(m2 - m1)
    w1 = 1.0 / (1.0 + e)
    ti_ref[0] = jnp.concatenate([i1, i2], axis=0).astype(I32)
    tw_ref[0] = jnp.concatenate([w1, e * w1], axis=0)


def _merge_call(attn, vcu, bg, x, wout, convw, gout, gpost, gt, gpre, sc, sh, w_router, name):
    b, s, d = x.shape
    cw = attn.shape[-1]
    tm = min(TOKEN_TILE, s)
    nt = s // tm
    moe = w_router is not None
    n_exp = w_router.shape[-1] if moe else 0
    if moe:
        wr = jnp.zeros((EXPERT_ROWS, d), F32).at[:n_exp].set(w_router.T)
    per_batch = gt.shape[0] > 1
    hb = tm // HALO
    n_halo = s // HALO
    tile = lambda width: pl.BlockSpec((1, tm, width), lambda bi, i: (bi, i, 0))
    vec = lambda width: pl.BlockSpec((1, width), lambda bi, i: (0, 0))
    mod = pl.BlockSpec((1, 1, d), (lambda bi, i: (bi, 0, 0)) if per_batch else (lambda bi, i: (0, 0, 0)))
    in_specs = [tile(cw), tile(cw),
                pl.BlockSpec((1, HALO, cw), lambda bi, i: (bi, jnp.maximum(i * hb - 1, 0), 0)),
                pl.BlockSpec((1, HALO, cw), lambda bi, i: (bi, jnp.minimum((i + 1) * hb, n_halo - 1), 0)),
                tile(cw), tile(d),
                pl.BlockSpec(wout.shape, lambda bi, i: (0, 0)),
                pl.BlockSpec(convw.shape, lambda bi, i: (0, 0)),
                vec(d), vec(d), mod, vec(d), mod, mod]
    args = [attn, vcu, vcu, vcu, bg, x, wout, convw, gout, gpost, gt, gpre, sc, sh]
    out_specs = [tile(d), tile(d)]
    out_shape = [jax.ShapeDtypeStruct((b, s, d), F32), jax.ShapeDtypeStruct((b, s, d), BF16)]
    if moe:
        r = d // LANES
        out_specs[1] = pl.BlockSpec((1, tm * r, LANES), lambda bi, i: (bi, i, 0))
        out_shape[1] = jax.ShapeDtypeStruct((b, s * r, LANES), F32)
        in_specs.append(pl.BlockSpec(wr.shape, lambda bi, i: (0, 0)))
        args.append(wr)
        route = pl.BlockSpec((1, TOP_K, tm), lambda bi, i: (bi, 0, i))
        out_specs += [route, route]
        out_shape += [jax.ShapeDtypeStruct((b, TOP_K, s), I32), jax.ShapeDtypeStruct((b, TOP_K, s), F32)]
    vmem = 2 * wout.size * 2 + 2 * tm * (3 * cw * 2 + 3 * d * 4) + 16 * tm * d * 4
    return pl.pallas_call(
        functools.partial(_merge_kernel, n_exp=n_exp),
        grid=(b, nt),
        in_specs=in_specs,
        out_specs=out_specs,
        out_shape=out_shape,
        compiler_params=_cparams(("parallel", "arbitrary"), vmem),
        name=name,
    )(*args)


def _swiglu(hb, wgu, wdn, ff):
    chunk = min(FF_CHUNK, ff)
    acc = None
    for c in range(ff // chunk):
        lo, hi = c * chunk, (c + 1) * chunk
        g = _dot(hb, wgu(lo, hi))
        u = _dot(hb, wgu(ff + lo, ff + hi))
        part = _dot((_silu(g) * u).astype(BF16), wdn(lo, hi))
        acc = part if acc is None else acc + part
    return acc


def _ffn_kernel(h_ref, x_ref, wgu_ref, wdn_ref, gpost_ref, gt_ref, o_ref):
    ff = wdn_ref.shape[0]
    y = _swiglu(h_ref[0], lambda lo, hi: wgu_ref[:, lo:hi], lambda lo, hi: wdn_ref[lo:hi, :], ff)
    o_ref[0] = x_ref[0] + gt_ref[0] * _rms(y, gpost_ref[...])


def _ffn_call(h, x, wgu, wdn, gpost, gt, name):
    b, s, d = x.shape
    tm = min(TOKEN_TILE, s)
    per_batch = gt.shape[0] > 1
    tile = pl.BlockSpec((1, tm, d), lambda bi, i: (bi, i, 0))
    vmem = (wgu.size + wdn.size) * 2 + 2 * tm * d * (2 + 4 + 4) + 10 * tm * d * 4
    return pl.pallas_call(
        _ffn_kernel,
        grid=(b, s // tm),
        in_specs=[tile, tile,
                  pl.BlockSpec(wgu.shape, lambda bi, i: (0, 0)),
                  pl.BlockSpec(wdn.shape, lambda bi, i: (0, 0)),
                  pl.BlockSpec((1, d), lambda bi, i: (0, 0)),
                  pl.BlockSpec((1, 1, d), (lambda bi, i: (bi, 0, 0)) if per_batch else (lambda bi, i: (0, 0, 0)))],
        out_specs=tile,
        out_shape=jax.ShapeDtypeStruct((b, s, d), F32),
        compiler_params=_cparams(("parallel", "parallel"), vmem),
        name=name,
    )(h, x, wgu, wdn, gpost, gt)


def _plan_kernel(ti_ref, rank_ref, cnt_ref, carry):
    @pl.when(pl.program_id(0) == 0)
    def _():
        carry[...] = jnp.zeros_like(carry)

    ti = ti_ref[...]
    tm = ti.shape[1]
    sub = lax.broadcasted_iota(I32, (EXPERT_ROWS, tm), 0)
    e1 = sub == ti[0:1, :]
    e2 = sub == ti[1:2, :]
    c = e1.astype(F32) + e2.astype(F32)
    tri = (lax.broadcasted_iota(I32, (tm, tm), 0) < lax.broadcasted_iota(I32, (tm, tm), 1)).astype(BF16)
    before = _dot(c.astype(BF16), tri) + carry[:, 0:1]
    r1 = jnp.where(e1, before, 0.0).sum(axis=0, keepdims=True)
    r2 = jnp.where(e2, before, 0.0).sum(axis=0, keepdims=True)
    rank_ref[...] = jnp.concatenate([r1, r2], axis=0).astype(I32)
    carry[...] += c.sum(axis=1, keepdims=True)
    cnt_ref[...] = carry[...]


def _plan_call(ti):
    n = ti.shape[1]
    tm = min(TOKEN_TILE, n)
    return pl.pallas_call(
        _plan_kernel,
        grid=(n // tm,),
        in_specs=[pl.BlockSpec((TOP_K, tm), lambda j: (0, j))],
        out_specs=[pl.BlockSpec((TOP_K, tm), lambda j: (0, j)),
                   pl.BlockSpec((EXPERT_ROWS, LANES), lambda j: (0, 0))],
        out_shape=[jax.ShapeDtypeStruct((TOP_K, n), I32), jax.ShapeDtypeStruct((EXPERT_ROWS, LANES), F32)],
        scratch_shapes=[pltpu.VMEM((EXPERT_ROWS, LANES), F32)],
        compiler_params=_cparams(("arbitrary",), 8 * tm * tm * 4 + 16 * tm * LANES * 4),
        name="route_plan",
    )(ti)


def _dispatch_kernel(p1_ref, p2_ref, pad_ref, h_ref, o_ref, zrow, sem, *, r):
    tm = h_ref.shape[0] // r
    j = pl.program_id(0)
    base = j * tm

    @pl.when(j == 0)
    def _():
        zrow[...] = jnp.zeros_like(zrow)
        for e in range(pad_ref.shape[0] // 2):
            lo, hi = pad_ref[2 * e], pad_ref[2 * e + 1]
            lax.fori_loop(lo, hi, lambda p, c: (_token_copy(zrow, 0, o_ref, p, r, sem).start(), c)[1], 0)
            lax.fori_loop(lo, hi, lambda p, c: (_token_copy(zrow, 0, o_ref, 0, r, sem).wait(), c)[1], 0)

    def issue(t, carry):
        _token_copy(h_ref, t, o_ref, p1_ref[base + t], r, sem).start(priority=0)
        _token_copy(h_ref, t, o_ref, p2_ref[base + t], r, sem).start(priority=1)
        return carry

    def drain(t, carry):
        _token_copy(h_ref, 0, o_ref, 0, r, sem).wait()
        _token_copy(h_ref, 0, o_ref, 0, r, sem).wait()
        return carry

    lax.fori_loop(0, tm, issue, 0, unroll=ROW_DMA_UNROLL)
    lax.fori_loop(0, tm, drain, 0, unroll=ROW_DMA_UNROLL)


def _dispatch_call(pos1, pos2, pad_ranges, h_tiles, n, n_rows):
    r = h_tiles.shape[0] // n
    tm = min(ROUTE_TILE, n)
    return pl.pallas_call(
        functools.partial(_dispatch_kernel, r=r),
        grid_spec=pltpu.PrefetchScalarGridSpec(
            num_scalar_prefetch=3,
            grid=(n // tm,),
            in_specs=[pl.BlockSpec((tm * r, LANES), lambda j, p1, p2, pad: (j, 0))],
            out_specs=pl.BlockSpec(memory_space=pl.ANY),
            scratch_shapes=[pltpu.VMEM((r, LANES), F32), pltpu.SemaphoreType.DMA(())]),
        out_shape=jax.ShapeDtypeStruct((n_rows * r, LANES), F32),
        compiler_params=_cparams(("arbitrary",), 6 * tm * r * LANES * 4),
        name="route_dispatch",
    )(pos1, pos2, pad_ranges, h_tiles)


def _experts_kernel(te_ref, tv_ref, h_ref, wgu_ref, wdn_ref, y_ref, *, tm):
    j = pl.program_id(0)
    ff = wdn_ref.shape[1]

    @pl.when(tv_ref[j] > 0)
    def _():
        y = _swiglu(_from_token_tiles(h_ref, tm).astype(BF16),
                    lambda lo, hi: wgu_ref[0, :, lo:hi],
                    lambda lo, hi: wdn_ref[0, lo:hi, :], ff)
        _to_token_tiles(y_ref, y)

    @pl.when(tv_ref[j] == 0)
    def _():
        y_ref[...] = jnp.zeros_like(y_ref)


def _experts_call(tile_expert, tile_valid, h_sorted, wgu, wdn, tm):
    n_exp, d, ff2 = wgu.shape
    ff = wdn.shape[1]
    r = d // LANES
    n_tiles = h_sorted.shape[0] // (tm * r)
    vmem = (wgu.size + wdn.size) // n_exp * 2 + 4 * tm * d * 4 + 12 * tm * d * 4
    return pl.pallas_call(
        functools.partial(_experts_kernel, tm=tm),
        grid_spec=pltpu.PrefetchScalarGridSpec(
            num_scalar_prefetch=2,
            grid=(n_tiles,),
            in_specs=[pl.BlockSpec((tm * r, LANES), lambda j, te, tv: (j, 0)),
                      pl.BlockSpec((1, d, ff2), lambda j, te, tv: (te[j], 0, 0),
                                   pipeline_mode=pl.Buffered(1)),
                      pl.BlockSpec((1, ff, d), lambda j, te, tv: (te[j], 0, 0),
                                   pipeline_mode=pl.Buffered(1))],
            out_specs=pl.BlockSpec((tm * r, LANES), lambda j, te, tv: (j, 0))),
        out_shape=jax.ShapeDtypeStruct(h_sorted.shape, F32),
        compiler_params=_cparams(("arbitrary",), vmem),
        name="route_experts",
    )(tile_expert, tile_valid, h_sorted, wgu, wdn)


def _combine_kernel(p1_ref, p2_ref, tw_ref, x_ref, gt_ref, gpost_ref, y_ref, o_ref, buf, sem, *, r):
    tm = x_ref.shape[0]
    j = pl.program_id(0)
    last = pl.num_programs(0) - 1
    slot = j % 2

    def gather_tile(tile, s):
        base = tile * tm

        def issue(t, carry):
            _token_copy(y_ref, p1_ref[base + t], buf.at[s, 0], t, r, sem.at[s]).start(priority=0)
            _token_copy(y_ref, p2_ref[base + t], buf.at[s, 1], t, r, sem.at[s]).start(priority=1)
            return carry

        lax.fori_loop(0, tm, issue, 0, unroll=ROW_DMA_UNROLL)

    @pl.when(j == 0)
    def _():
        gather_tile(0, 0)

    @pl.when(j < last)
    def _():
        gather_tile(j + 1, 1 - slot)

    def drain(t, carry):
        _token_copy(y_ref, 0, buf.at[slot, 0], 0, r, sem.at[slot]).wait()
        _token_copy(y_ref, 0, buf.at[slot, 1], 0, r, sem.at[slot]).wait()
        return carry

    lax.fori_loop(0, tm, drain, 0, unroll=ROW_DMA_UNROLL)
    tw = tw_ref[...]
    y = (tw[:, 0:1] * _from_token_tiles(buf.at[slot, 0], tm)
         + tw[:, 1:2] * _from_token_tiles(buf.at[slot, 1], tm))
    o_ref[...] = x_ref[...] + gt_ref[0] * _rms(y, gpost_ref[...])


def _combine_call(pos1, pos2, tw, x, gt, gpost, y_sorted, seq):
    n, d = x.shape
    r = d // LANES
    tm = min(ROUTE_TILE, seq)
    per_seq = seq // tm
    return pl.pallas_call(
        functools.partial(_combine_kernel, r=r),
        grid_spec=pltpu.PrefetchScalarGridSpec(
            num_scalar_prefetch=2,
            grid=(n // tm,),
            in_specs=[pl.BlockSpec((tm, TOP_K), lambda j, p1, p2: (j, 0)),
                      pl.BlockSpec((tm, d), lambda j, p1, p2: (j, 0)),
                      pl.BlockSpec((1, 1, d), lambda j, p1, p2: (j // per_seq, 0, 0)),
                      pl.BlockSpec((1, d), lambda j, p1, p2: (0, 0)),
                      pl.BlockSpec(memory_space=pl.ANY)],
            out_specs=pl.BlockSpec((tm, d), lambda j, p1, p2: (j, 0)),
            scratch_shapes=[pltpu.VMEM((2, TOP_K, tm * r, LANES), F32), pltpu.SemaphoreType.DMA((2,))]),
        out_shape=jax.ShapeDtypeStruct((n, d), F32),
        compiler_params=_cparams(("arbitrary",), 16 * tm * d * 4),
        name="route_combine",
    )(pos1, pos2, tw, x, gt, gpost, y_sorted)


def _moe_call(h, ti, tw, x, gt, gpost, wgu, wdn):
    b, s, d = x.shape
    n = b * s
    n_exp = wgu.shape[0]
    tm = TOKEN_TILE
    ti = ti.transpose(1, 0, 2).reshape(TOP_K, n)
    rank, cnt = _plan_call(ti)
    counts = cnt[:n_exp, 0].astype(I32)
    padded = (counts + tm - 1) // tm * tm
    ends = jnp.cumsum(padded)
    starts = ends - padded
    experts = jnp.arange(n_exp, dtype=I32)
    pos = rank + jnp.sum(jnp.where(ti[..., None] == experts, starts, 0), axis=-1)
    n_tiles = n * TOP_K // tm + n_exp
    n_rows = n_tiles * tm
    row0 = jnp.arange(n_tiles, dtype=I32) * tm
    tile_expert = jnp.minimum(jnp.sum(row0[:, None] >= ends[None, :], axis=1), n_exp - 1).astype(I32)
    tile_valid = (row0 < ends[-1]).astype(I32)
    pad_lo = jnp.concatenate([starts + counts, ends[-1:]])
    pad_hi = jnp.concatenate([ends, jnp.full((1,), n_rows, I32)])
    pad_ranges = jnp.stack([pad_lo, pad_hi], axis=1).reshape(-1).astype(I32)
    pos1, pos2 = pos[0], pos[1]
    h_sorted = _dispatch_call(pos1, pos2, pad_ranges, h.reshape(-1, LANES), n, n_rows)
    y_sorted = _experts_call(tile_expert, tile_valid, h_sorted, wgu, wdn, tm)
    tw = tw.transpose(0, 2, 1).reshape(n, TOP_K)
    out = _combine_call(pos1, pos2, tw, x.reshape(n, d), gt, gpost, y_sorted, s)
    return out.reshape(b, s, d)


def kernel(x, c, ctx, c_ctx, w_ada, b_ada, norm_g, w_in, rpb, conv_w, out_norm_g, w_out,
           w_gu_dense, w_down_dense, w_router, w_gu_moe, w_down_moe):
    depth = w_ada.shape[0]
    b, s, d = x.shape
    cw = d // 2

    cc =jnp.zeros((16, d), F32).at[:b].set(c).at[b].set(c_ctx)
    mods = _ada_call(cc, w_ada, b_ada)

    for l in range(depth):
        last = l == depth - 1
        m = mods[l].reshape(16, 6, d)
        lat = [m[:b, j][:, None, :] for j in range(6)]
        cxm = [m[b:b + 1, j][:, None, :] for j in range(6)]
        g = [norm_g[l, j][None, :] for j in range(4)]
        w_in_l = w_in[l].astype(BF16)
        w_out_l = w_out[l].astype(BF16)
        gout = out_norm_g[l][None, :]

        q, k, v, bg, vcu = _inproj_call(x, g[0], lat[1], lat[0], w_in_l, True, f"inproj_x{l}")
        if last:
            kc, vc = _inproj_call(ctx, g[0], cxm[1], cxm[0], w_in_l[:, cw:3 * cw], False, f"inproj_c{l}")
        else:
            qc, kc, vc, bgc, vcuc = _inproj_call(ctx, g[0], cxm[1], cxm[0], w_in_l, True, f"inproj_c{l}")

        attn = _na_call(q, k, v, kc, vc, *_bias_tables(rpb[l]))
        moe = l % 2 == 1
        x, h, *route = _merge_call(attn, vcu, bg, x, w_out_l, conv_w[l], gout, g[1], lat[2], g[2],
                                   lat[4], lat[3], w_router[l // 2] if moe else None, f"merge_x{l}")
        if not last:
            attn_c = _ctx_attn_call(qc, kc, vc)
            ctx, hc = _merge_call(attn_c, vcuc, bgc, ctx, w_out_l, conv_w[l], gout, g[1], cxm[2], g[2],
                                  cxm[4], cxm[3], None, f"merge_c{l}")

        if moe:
            x = _moe_call(h, route[0], route[1], x, lat[5], g[3],
                          w_gu_moe[l // 2].astype(BF16), w_down_moe[l // 2].astype(BF16))
        else:
            wgu = w_gu_dense[l // 2].astype(BF16)
            wdn = w_down_dense[l // 2].astype(BF16)
            x = _ffn_call(h, x, wgu, wdn, g[3], lat[5], f"ffn_x{l}")
            if not last:
                ctx = _ffn_call(hc, ctx, wgu, wdn, g[3], cxm[5], f"ffn_c{l}")
    return x
```

```python
import functools

import numpy as np
import jax
import jax.numpy as jnp
from jax import lax
from jax.experimental import pallas as pl
from jax.experimental.pallas import tpu as pltpu

F32 = jnp.float32
BF16 = jnp.bfloat16
I32 = jnp.int32

EPS = 1e-6
GRID_W = 64
HEAD_DIM = 64
NA_KH = 8
NA_KW = 16
TOP_K = 2
LANES = 128
Q_ROWS = 4
BAND_ROWS = 12
ATTN_BLOCKS_PER_STEP = 4
DY_PAD = Q_ROWS
NEG = -1e30
HALO = 16
V7X_VMEM_CAP = 60 * 1024 * 1024
TOKEN_TILE = 512
ROUTE_TILE = 1024
FF_CHUNK = 512
ROW_DMA_UNROLL = 8
EXPERT_ROWS = 16


def _cparams(sem, vmem_bytes):
    return pltpu.CompilerParams(dimension_semantics=sem,
                                vmem_limit_bytes=int(min(vmem_bytes, V7X_VMEM_CAP)))


def _rms(x, g):
    return x * lax.rsqrt(jnp.mean(x * x, axis=-1, keepdims=True) + EPS) * g


def _dot(a, b):
    return jnp.dot(a, b, preferred_element_type=F32)


def _dot_nt(a, b):
    return lax.dot_general(a, b, (((1,), (1,)), ((), ())), preferred_element_type=F32)


def _split_bf16(a):
    hi = a.astype(BF16)
    lo = (a - hi.astype(F32)).astype(BF16)
    return hi, lo


def _dot3(a, w):
    ah, al = _split_bf16(a)
    wh, wl = _split_bf16(w)
    return _dot(ah, wh) + (_dot(ah, wl) + _dot(al, wh))


def _silu(x):
    return x / (1.0 + jnp.exp(-x))


def _to_token_tiles(ref, val):
    rows, d = val.shape
    r = d // LANES
    for c in range(r):
        ref[pl.ds(c, rows, stride=r), :] = val[:, c * LANES:(c + 1) * LANES]


def _from_token_tiles(ref, rows):
    r = ref.shape[0] // rows
    return jnp.concatenate([ref[pl.ds(c, rows, stride=r), :] for c in range(r)], axis=1)


def _token_copy(src, src_tok, dst, dst_tok, r, sem):
    s0 = pl.multiple_of(src_tok * r, r)
    d0 = pl.multiple_of(dst_tok * r, r)
    return pltpu.make_async_copy(src.at[pl.ds(s0, r), :], dst.at[pl.ds(d0, r), :], sem)


def _ada_kernel(c_ref, w_ref, b_ref, o_ref):
    o_ref[0] = _dot3(_silu(c_ref[...]), w_ref[0]) + b_ref[0]


def _ada_call(cc, w_ada, b_ada):
    depth, d, n = w_ada.shape
    rows = cc.shape[0]
    tn = n // 4
    return pl.pallas_call(
        _ada_kernel,
        grid=(depth, n // tn),
        in_specs=[pl.BlockSpec((rows, d), lambda l, j: (0, 0)),
                  pl.BlockSpec((1, d, tn), lambda l, j: (l, 0, j)),
                  pl.BlockSpec((1, 1, tn), lambda l, j: (l, 0, j))],
        out_specs=pl.BlockSpec((1, rows, tn), lambda l, j: (l, 0, j)),
        out_shape=jax.ShapeDtypeStruct((depth, rows, n), F32),
        compiler_params=_cparams(("parallel", "parallel"), 6 * d * tn * 4),
        name="adaln",
    )(cc, w_ada, b_ada.reshape(depth, 1, n))


def _inproj_kernel(x_ref, g_ref, sc_ref, sh_ref, w_ref, *out_refs, full):
    h = _rms(x_ref[0], g_ref[...]) * (1.0 + sc_ref[0]) + sh_ref[0]
    hb = h.astype(BF16)
    cw = out_refs[0].shape[-1]

    def col(j):
        return _dot(hb, w_ref[:, j * cw:(j + 1) * cw])

    if full:
        q_ref, k_ref, v_ref, bg_ref, vcu_ref = out_refs
        q_ref[0] = col(0).astype(BF16)
        k_ref[0] = col(1).astype(BF16)
        v_ref[0] = col(2).astype(BF16)
        bg_ref[0] = col(3).astype(BF16)
        vcu_ref[0] = (col(4) * col(5)).astype(BF16)
    else:
        k_ref, v_ref = out_refs
        k_ref[0] = col(0).astype(BF16)
        v_ref[0] = col(1).astype(BF16)


def _inproj_call(x, g, sc, sh, w, full, name):
    b, s, d = x.shape
    cw = d // 2
    tm = min(TOKEN_TILE, s)
    n_out = 5 if full else 2
    per_batch = sc.shape[0] > 1
    mod_spec = pl.BlockSpec((1, 1, d), (lambda bi, i: (bi, 0, 0)) if per_batch else (lambda bi, i: (0, 0, 0)))
    out_spec = pl.BlockSpec((1, tm, cw), lambda bi, i: (bi, i, 0))
    vmem = 2 * tm * d * 4 + 2 * w.size * 2 + 2 * n_out * tm * cw * 2 + 8 * tm * d * 4
    return pl.pallas_call(
        functools.partial(_inproj_kernel, full=full),
        grid=(b, s // tm),
        in_specs=[pl.BlockSpec((1, tm, d), lambda bi, i: (bi, i, 0)),
                  pl.BlockSpec((1, d), lambda bi, i: (0, 0)),
                  mod_spec, mod_spec,
                  pl.BlockSpec(w.shape, lambda bi, i: (0, 0))],
        out_specs=[out_spec] * n_out,
        out_shape=[jax.ShapeDtypeStruct((b, s, cw), BF16)] * n_out,
        compiler_params=_cparams(("parallel", "parallel"), vmem),
        name=name,
    )(x, g, sc, sh, w)


def _head_pair_attention(q2, key_parts, value_parts, bias_parts):
    m_rows = q2.shape[0]
    lane = lax.broadcasted_iota(I32, q2.shape, 1)
    qs = q2 * (HEAD_DIM ** -0.5)
    qq = jnp.concatenate([jnp.where(lane < HEAD_DIM, qs, 0), jnp.where(lane >= HEAD_DIM, qs, 0)], axis=0)
    scores = []
    for kk, bias in zip(key_parts, bias_parts):
        s = _dot_nt(qq, kk)
        scores.append(s if bias is None else s + bias)
    m = scores[0].max(axis=-1, keepdims=True)
    for s in scores[1:]:
        m = jnp.maximum(m, s.max(axis=-1, keepdims=True))
    acc = None
    for s, vv in zip(scores, value_parts):
        v_aug = jnp.concatenate([vv, jnp.ones_like(vv)], axis=1)
        acc_i = _dot(jnp.exp(s - m).astype(BF16), v_aug)
        acc = acc_i if acc is None else acc + acc_i
    out = acc[:, :LANES] / acc[:, LANES:]
    return jnp.where(lane < HEAD_DIM, out[:m_rows], out[m_rows:])


def _na_kernel(q_ref, k_ref, v_ref, kc_ref, vc_ref, bias_ref, rmask_ref, o_ref, *, grid_rows, blocks_per_step):
    tq = Q_ROWS * GRID_W
    band = BAND_ROWS * GRID_W
    n_blocks = grid_rows // Q_ROWS
    n_pairs = q_ref.shape[-1] // LANES
    off_interior = NA_KH // 2 - 1 + DY_PAD
    off_first = NA_KH - 1 + DY_PAD
    off_last = Q_ROWS - BAND_ROWS + NA_KH - 1 + DY_PAD
    for jb in range(blocks_per_step):
        ib = pl.program_id(1) * blocks_per_step + jb
        start_row = jnp.clip(Q_ROWS * ib - NA_KH // 2, 0, grid_rows - BAND_ROWS)
        tok0 = pl.multiple_of(start_row * GRID_W, GRID_W)
        rows = slice(jb * tq, (jb + 1) * tq)
        may_be_edge = jb == 0 or jb == blocks_per_step - 1
        if may_be_edge:
            kind = jnp.where(ib == 0, 0, jnp.where(ib == n_blocks - 1, 2, 1))
            variant = jnp.where(kind == 1, 0, 1)
            off = jnp.where(kind == 1, off_interior, jnp.where(kind == 0, off_first, off_last))
        else:
            variant, off = 0, off_interior
        for p in range(n_pairs):
            ls = slice(p * LANES, (p + 1) * LANES)
            k2 = k_ref[0, pl.ds(tok0, band), ls]
            v2 = v_ref[0, pl.ds(tok0, band), ls]
            bias_rows = []
            for hh in range(2):
                for jq in range(Q_ROWS):
                    blocks = []
                    for ap in range(BAND_ROWS // 2):
                        blk = bias_ref[variant, 2 * p + hh, 2 * ap - jq + off]
                        blocks.append(blk + rmask_ref[kind, ap] if may_be_edge else blk)
                    bias_rows.append(jnp.concatenate(blocks, axis=1))
            o2 = _head_pair_attention(q_ref[0, rows, ls], [k2, kc_ref[0, :, ls]], [v2, vc_ref[0, :, ls]],
                                      [jnp.concatenate(bias_rows, axis=0), None])
            o_ref[0, rows, ls] = o2.astype(BF16)


def _na_call(q, k, v, kc, vc, bias, row_mask):
    b, s, cw = q.shape
    ctx = kc.shape[1]
    grid_rows = s // GRID_W
    tq = Q_ROWS * GRID_W
    nblk = s // tq
    bps = min(ATTN_BLOCKS_PER_STEP, nblk)
    band = BAND_ROWS * GRID_W
    vmem = (4 * s * cw * 2 + 4 * ctx * cw * 2 + bias.size * 4
            + 4 * bps * tq * cw * 2 + 16 * tq * (band + ctx) * 4)
    return pl.pallas_call(
        functools.partial(_na_kernel, grid_rows=grid_rows, blocks_per_step=bps),
        grid=(b, nblk // bps),
        in_specs=[pl.BlockSpec((1, bps * tq, cw), lambda bi, i: (bi, i, 0)),
                  pl.BlockSpec((1, s, cw), lambda bi, i: (bi, 0, 0)),
                  pl.BlockSpec((1, s, cw), lambda bi, i: (bi, 0, 0)),
                  pl.BlockSpec((1, ctx, cw), lambda bi, i: (bi, 0, 0)),
                  pl.BlockSpec((1, ctx, cw), lambda bi, i: (bi, 0, 0)),
                  pl.BlockSpec(bias.shape, lambda bi, i: (0, 0, 0, 0, 0)),
                  pl.BlockSpec(row_mask.shape, lambda bi, i: (0, 0, 0, 0))],
        out_specs=pl.BlockSpec((1, bps * tq, cw), lambda bi, i: (bi, i, 0)),
        out_shape=jax.ShapeDtypeStruct((b, s, cw), BF16),
        compiler_params=_cparams(("parallel", "arbitrary"), vmem),
        name="neigh_attn",
    )(q, k, v, kc, vc, bias, row_mask)


def _ctx_attn_kernel(q_ref, k_ref, v_ref, o_ref):
    n_pairs = q_ref.shape[-1] // LANES
    for p in range(n_pairs):
        ls = slice(p * LANES, (p + 1) * LANES)
        o2 = _head_pair_attention(q_ref[0, :, ls], [k_ref[0, :, ls]], [v_ref[0, :, ls]], [None])
        o_ref[0, :, ls] = o2.astype(BF16)


def _ctx_attn_call(q, k, v):
    b, s, cw = q.shape
    spec = pl.BlockSpec((1, s, cw), lambda bi: (bi, 0, 0))
    return pl.pallas_call(
        _ctx_attn_kernel,
        grid=(b,),
        in_specs=[spec, spec, spec],
        out_specs=spec,
        out_shape=jax.ShapeDtypeStruct((b, s, cw), BF16),
        compiler_params=_cparams(("parallel",), 8 * s * cw * 2 + 16 * s * s * 4),
        name="ctx_attn",
    )(q, k, v)


def _bias_tables(rpb_l):
    w = GRID_W
    n_heads, n_dy, n_dx = rpb_l.shape
    qc = np.arange(w)[:, None]
    kc = np.arange(w)[None, :]
    cs = np.clip(qc - NA_KW // 2, 0, w - NA_KW)
    ok_c = (kc >= cs) & (kc < cs + NA_KW)
    sel_x = ok_c[..., None] & ((kc - qc + NA_KW - 1)[..., None] == np.arange(n_dx))
    cols = jnp.einsum("hyx,wcx->hywc", rpb_l, jnp.asarray(sel_x, F32), precision=lax.Precision.HIGHEST)
    cols = jnp.where(jnp.asarray(ok_c), cols, NEG)
    dy = np.arange(n_dy)
    inner = (dy >= NA_KH // 2 - 1) & (dy < NA_KH // 2 - 1 + NA_KH)
    neg = jnp.full((n_heads, DY_PAD, w, w), NEG, F32)
    slots = jnp.stack([
        jnp.concatenate([neg, jnp.where(jnp.asarray(inner)[None, :, None, None], cols, NEG), neg], axis=1),
        jnp.concatenate([neg, cols, neg], axis=1)])
    blocks = jnp.concatenate([slots[:, :, :-1], slots[:, :, 1:]], axis=-1)
    a = np.arange(BAND_ROWS)
    ok_a = np.stack([a < NA_KH, a >= 0, a >= BAND_ROWS - NA_KH])
    row_mask = np.where(ok_a, 0.0, NEG).astype(np.float32).reshape(3, BAND_ROWS // 2, 2, 1)
    row_mask = np.broadcast_to(row_mask, (3, BAND_ROWS // 2, 2, w)).reshape(3, BAND_ROWS // 2, 1, 2 * w)
    return blocks, jnp.asarray(row_mask)


def _merge_kernel(attn_ref, vcu_ref, vprev_ref, vnext_ref, bg_ref, x_ref, wout_ref, convw_ref,
                  gout_ref, gpost_ref, gt_ref, gpre_ref, sc_ref, sh_ref, *rest, n_exp):
    moe = n_exp > 0
    i = pl.program_id(1)
    n = pl.num_programs(1)
    v = vcu_ref[0].astype(F32)
    tm, cw = v.shape
    row = lax.broadcasted_iota(I32, v.shape, 0)
    prev = jnp.where(i > 0, vprev_ref[0, HALO - 1:HALO, :].astype(F32), 0.0)
    nxt = jnp.where(i < n - 1, vnext_ref[0, 0:1, :].astype(F32), 0.0)
    v_m1 = jnp.where(row == 0, prev, pltpu.roll(v, 1, 0))
    v_p1 = jnp.where(row == tm - 1, nxt, pltpu.roll(v, tm - 1, 0))
    w3 = convw_ref[...]
    conv = w3[0:1] * v_m1 + w3[1:2] * v + w3[2:3] * v_p1
    cv = bg_ref[0].astype(F32) * conv
    g_out = gout_ref[...]
    a = _rms(attn_ref[0].astype(F32), g_out[:, :cw])
    c = _rms(cv, g_out[:, cw:])
    y = _dot(a.astype(BF16), wout_ref[:cw, :]) + _dot(c.astype(BF16), wout_ref[cw:, :])
    xn = x_ref[0] + _rms(y, gt_ref[0] * gpost_ref[...])
    h = _rms(xn, gpre_ref[...] * (1.0 + sc_ref[0])) + sh_ref[0]
    if not moe:
        x_out_ref, h_ref = rest
        x_out_ref[0] = xn
        h_ref[0] = h.astype(BF16)
        return
    wr_ref, x_out_ref, h_ref, ti_ref, tw_ref = rest
    x_out_ref[0] = xn
    _to_token_tiles(h_ref.at[0], h)
    hh, hl = _split_bf16(h)
    wh, wl = _split_bf16(wr_ref[...])
    both = _dot_nt(jnp.concatenate([wh, wl], axis=0), hh)
    logits = both[:EXPERT_ROWS] + (both[EXPERT_ROWS:] + _dot_nt(wh, hl))
    sub = lax.broadcasted_iota(I32, logits.shape, 0).astype(F32)
    logits = jnp.where(sub < n_exp, logits, NEG)
    big = float(EXPERT_ROWS)
    m1 = logits.max(axis=0, keepdims=True)
    i1 = jnp.where(logits == m1, sub, big).min(axis=0, keepdims=True)
    l2 = jnp.where(sub == i1, NEG, logits)
    m2 = l2.max(axis=0, keepdims=True)
    i2 = jnp.where(l2 == m2, sub, big).min(axis=0, keepdims=True)
    e = jnp.exp(m2 - m1)
    w1 = 1.0 / (1.0 + e)
    ti_ref[0] = jnp.concatenate([i1, i2], axis=0).astype(I32)
    tw_ref[0] = jnp.concatenate([w1, e * w1], axis=0)


def _merge_call(attn, vcu, bg, x, wout, convw, gout, gpost, gt, gpre, sc, sh, w_router, name):
    b, s, d = x.shape
    cw = attn.shape[-1]
    tm = min(TOKEN_TILE, s)
    nt = s // tm
    moe = w_router is not None
    n_exp = w_router.shape[-1] if moe else 0
    if moe:
        wr = jnp.zeros((EXPERT_ROWS, d), F32).at[:n_exp].set(w_router.T)
    per_batch = gt.shape[0] > 1
    hb = tm // HALO
    n_halo = s // HALO
    tile = lambda width: pl.BlockSpec((1, tm, width), lambda bi, i: (bi, i, 0))
    vec = lambda width: pl.BlockSpec((1, width), lambda bi, i: (0, 0))
    mod = pl.BlockSpec((1, 1, d), (lambda bi, i: (bi, 0, 0)) if per_batch else (lambda bi, i: (0, 0, 0)))
    in_specs = [tile(cw), tile(cw),
                pl.BlockSpec((1, HALO, cw), lambda bi, i: (bi, jnp.maximum(i * hb - 1, 0), 0)),
                pl.BlockSpec((1, HALO, cw), lambda bi, i: (bi, jnp.minimum((i + 1) * hb, n_halo - 1), 0)),
                tile(cw), tile(d),
                pl.BlockSpec(wout.shape, lambda bi, i: (0, 0)),
                pl.BlockSpec(convw.shape, lambda bi, i: (0, 0)),
                vec(d), vec(d), mod, vec(d), mod, mod]
    args = [attn, vcu, vcu, vcu, bg, x, wout, convw, gout, gpost, gt, gpre, sc, sh]
    out_specs = [tile(d), tile(d)]
    out_shape = [jax.ShapeDtypeStruct((b, s, d), F32), jax.ShapeDtypeStruct((b, s, d), BF16)]
    if moe:
        r = d // LANES
        out_specs[1] = pl.BlockSpec((1, tm * r, LANES), lambda bi, i: (bi, i, 0))
        out_shape[1] = jax.ShapeDtypeStruct((b, s * r, LANES), F32)
        in_specs.append(pl.BlockSpec(wr.shape, lambda bi, i: (0, 0)))
        args.append(wr)
        route = pl.BlockSpec((1, TOP_K, tm), lambda bi, i: (bi, 0, i))
        out_specs += [route, route]
        out_shape += [jax.ShapeDtypeStruct((b, TOP_K, s), I32), jax.ShapeDtypeStruct((b, TOP_K, s), F32)]
    vmem = 2 * wout.size * 2 + 2 * tm * (3 * cw * 2 + 3 * d * 4) + 16 * tm * d * 4
    return pl.pallas_call(
        functools.partial(_merge_kernel, n_exp=n_exp),
        grid=(b, nt),
        in_specs=in_specs,
        out_specs=out_specs,
        out_shape=out_shape,
        compiler_params=_cparams(("parallel", "arbitrary"), vmem),
        name=name,
    )(*args)


def _swiglu(hb, wgu, wdn, ff):
    chunk = min(FF_CHUNK, ff)
    acc = None
    for c in range(ff // chunk):
        lo, hi = c * chunk, (c + 1) * chunk
        g = _dot(hb, wgu(lo, hi))
        u = _dot(hb, wgu(ff + lo, ff + hi))
        part = _dot((_silu(g) * u).astype(BF16), wdn(lo, hi))
        acc = part if acc is None else acc + part
    return acc


def _ffn_kernel(h_ref, x_ref, wgu_ref, wdn_ref, gpost_ref, gt_ref, o_ref):
    ff = wdn_ref.shape[0]
    y = _swiglu(h_ref[0], lambda lo, hi: wgu_ref[:, lo:hi], lambda lo, hi: wdn_ref[lo:hi, :], ff)
    o_ref[0] = x_ref[0] + gt_ref[0] * _rms(y, gpost_ref[...])


def _ffn_call(h, x, wgu, wdn, gpost, gt, name):
    b, s, d = x.shape
    tm = min(TOKEN_TILE, s)
    per_batch = gt.shape[0] > 1
    tile = pl.BlockSpec((1, tm, d), lambda bi, i: (bi, i, 0))
    vmem = (wgu.size + wdn.size) * 2 + 2 * tm * d * (2 + 4 + 4) + 10 * tm * d * 4
    return pl.pallas_call(
        _ffn_kernel,
        grid=(b, s // tm),
        in_specs=[tile, tile,
                  pl.BlockSpec(wgu.shape, lambda bi, i: (0, 0)),
                  pl.BlockSpec(wdn.shape, lambda bi, i: (0, 0)),
                  pl.BlockSpec((1, d), lambda bi, i: (0, 0)),
                  pl.BlockSpec((1, 1, d), (lambda bi, i: (bi, 0, 0)) if per_batch else (lambda bi, i: (0, 0, 0)))],
        out_specs=tile,
        out_shape=jax.ShapeDtypeStruct((b, s, d), F32),
        compiler_params=_cparams(("parallel", "parallel"), vmem),
        name=name,
    )(h, x, wgu, wdn, gpost, gt)


def _plan_kernel(ti_ref, rank_ref, cnt_ref, carry):
    @pl.when(pl.program_id(0) == 0)
    def _():
        carry[...] = jnp.zeros_like(carry)

    ti = ti_ref[...]
    tm = ti.shape[1]
    sub = lax.broadcasted_iota(I32, (EXPERT_ROWS, tm), 0)
    e1 = sub == ti[0:1, :]
    e2 = sub == ti[1:2, :]
    c = e1.astype(F32) + e2.astype(F32)
    tri = (lax.broadcasted_iota(I32, (tm, tm), 0) < lax.broadcasted_iota(I32, (tm, tm), 1)).astype(BF16)
    before = _dot(c.astype(BF16), tri) + carry[:, 0:1]
    r1 = jnp.where(e1, before, 0.0).sum(axis=0, keepdims=True)
    r2 = jnp.where(e2, before, 0.0).sum(axis=0, keepdims=True)
    rank_ref[...] = jnp.concatenate([r1, r2], axis=0).astype(I32)
    carry[...] += c.sum(axis=1, keepdims=True)
    cnt_ref[...] = carry[...]


def _plan_call(ti):
    n = ti.shape[1]
    tm = min(TOKEN_TILE, n)
    return pl.pallas_call(
        _plan_kernel,
        grid=(n // tm,),
        in_specs=[pl.BlockSpec((TOP_K, tm), lambda j: (0, j))],
        out_specs=[pl.BlockSpec((TOP_K, tm), lambda j: (0, j)),
                   pl.BlockSpec((EXPERT_ROWS, LANES), lambda j: (0, 0))],
        out_shape=[jax.ShapeDtypeStruct((TOP_K, n), I32), jax.ShapeDtypeStruct((EXPERT_ROWS, LANES), F32)],
        scratch_shapes=[pltpu.VMEM((EXPERT_ROWS, LANES), F32)],
        compiler_params=_cparams(("arbitrary",), 8 * tm * tm * 4 + 16 * tm * LANES * 4),
        name="route_plan",
    )(ti)


def _dispatch_kernel(p1_ref, p2_ref, pad_ref, h_ref, o_ref, zrow, sem, *, r):
    tm = h_ref.shape[0] // r
    j = pl.program_id(0)
    base = j * tm

    @pl.when(j == 0)
    def _():
        zrow[...] = jnp.zeros_like(zrow)
        for e in range(pad_ref.shape[0] // 2):
            lo, hi = pad_ref[2 * e], pad_ref[2 * e + 1]
            lax.fori_loop(lo, hi, lambda p, c: (_token_copy(zrow, 0, o_ref, p, r, sem).start(), c)[1], 0)
            lax.fori_loop(lo, hi, lambda p, c: (_token_copy(zrow, 0, o_ref, 0, r, sem).wait(), c)[1], 0)

    def issue(t, carry):
        _token_copy(h_ref, t, o_ref, p1_ref[base + t], r, sem).start(priority=0)
        _token_copy(h_ref, t, o_ref, p2_ref[base + t], r, sem).start(priority=1)
        return carry

    def drain(t, carry):
        _token_copy(h_ref, 0, o_ref, 0, r, sem).wait()
        _token_copy(h_ref, 0, o_ref, 0, r, sem).wait()
        return carry

    lax.fori_loop(0, tm, issue, 0, unroll=ROW_DMA_UNROLL)
    lax.fori_loop(0, tm, drain, 0, unroll=ROW_DMA_UNROLL)


def _dispatch_call(pos1, pos2, pad_ranges, h_tiles, n, n_rows):
    r = h_tiles.shape[0] // n
    tm = min(ROUTE_TILE, n)
    return pl.pallas_call(
        functools.partial(_dispatch_kernel, r=r),
        grid_spec=pltpu.PrefetchScalarGridSpec(
            num_scalar_prefetch=3,
            grid=(n // tm,),
            in_specs=[pl.BlockSpec((tm * r, LANES), lambda j, p1, p2, pad: (j, 0))],
            out_specs=pl.BlockSpec(memory_space=pl.ANY),
            scratch_shapes=[pltpu.VMEM((r, LANES), F32), pltpu.SemaphoreType.DMA(())]),
        out_shape=jax.ShapeDtypeStruct((n_rows * r, LANES), F32),
        compiler_params=_cparams(("arbitrary",), 6 * tm * r * LANES * 4),
        name="route_dispatch",
    )(pos1, pos2, pad_ranges, h_tiles)


def _experts_kernel(te_ref, tv_ref, h_ref, wgu_ref, wdn_ref, y_ref, *, tm):
    j = pl.program_id(0)
    ff = wdn_ref.shape[1]

    @pl.when(tv_ref[j] > 0)
    def _():
        y = _swiglu(_from_token_tiles(h_ref, tm).astype(BF16),
                    lambda lo, hi: wgu_ref[0, :, lo:hi],
                    lambda lo, hi: wdn_ref[0, lo:hi, :], ff)
        _to_token_tiles(y_ref, y)

    @pl.when(tv_ref[j] == 0)
    def _():
        y_ref[...] = jnp.zeros_like(y_ref)


def _experts_call(tile_expert, tile_valid, h_sorted, wgu, wdn, tm):
    n_exp, d, ff2 = wgu.shape
    ff = wdn.shape[1]
    r = d // LANES
    n_tiles = h_sorted.shape[0] // (tm * r)
    vmem = (wgu.size + wdn.size) // n_exp * 2 + 4 * tm * d * 4 + 12 * tm * d * 4
    return pl.pallas_call(
        functools.partial(_experts_kernel, tm=tm),
        grid_spec=pltpu.PrefetchScalarGridSpec(
            num_scalar_prefetch=2,
            grid=(n_tiles,),
            in_specs=[pl.BlockSpec((tm * r, LANES), lambda j, te, tv: (j, 0)),
                      pl.BlockSpec((1, d, ff2), lambda j, te, tv: (te[j], 0, 0),
                                   pipeline_mode=pl.Buffered(1)),
                      pl.BlockSpec((1, ff, d), lambda j, te, tv: (te[j], 0, 0))],
            out_specs=pl.BlockSpec((tm * r, LANES), lambda j, te, tv: (j, 0))),
        out_shape=jax.ShapeDtypeStruct(h_sorted.shape, F32),
        compiler_params=_cparams(("arbitrary",), vmem),
        name="route_experts",
    )(tile_expert, tile_valid, h_sorted, wgu, wdn)


def _combine_kernel(p1_ref, p2_ref, tw_ref, x_ref, gt_ref, gpost_ref, y_ref, o_ref, buf, sem, *, r):
    tm = x_ref.shape[0]
    j = pl.program_id(0)
    last = pl.num_programs(0) - 1
    slot = j % 2

    def gather_tile(tile, s):
        base = tile * tm

        def issue(t, carry):
            _token_copy(y_ref, p1_ref[base + t], buf.at[s, 0], t, r, sem.at[s]).start(priority=0)
            _token_copy(y_ref, p2_ref[base + t], buf.at[s, 1], t, r, sem.at[s]).start(priority=1)
            return carry

        lax.fori_loop(0, tm, issue, 0, unroll=ROW_DMA_UNROLL)

    @pl.when(j == 0)
    def _():
        gather_tile(0, 0)

    @pl.when(j < last)
    def _():
        gather_tile(j + 1, 1 - slot)

    def drain(t, carry):
        _token_copy(y_ref, 0, buf.at[slot, 0], 0, r, sem.at[slot]).wait()
        _token_copy(y_ref, 0, buf.at[slot, 1], 0, r, sem.at[slot]).wait()
        return carry

    lax.fori_loop(0, tm, drain, 0, unroll=ROW_DMA_UNROLL)
    tw = tw_ref[...]
    y = (tw[:, 0:1] * _from_token_tiles(buf.at[slot, 0], tm)
         + tw[:, 1:2] * _from_token_tiles(buf.at[slot, 1], tm))
    o_ref[...] = x_ref[...] + gt_ref[0] * _rms(y, gpost_ref[...])


def _combine_call(pos1, pos2, tw, x, gt, gpost, y_sorted, seq):
    n, d = x.shape
    r = d // LANES
    tm = min(ROUTE_TILE, seq)
    per_seq = seq // tm
    return pl.pallas_call(
        functools.partial(_combine_kernel, r=r),
        grid_spec=pltpu.PrefetchScalarGridSpec(
            num_scalar_prefetch=2,
            grid=(n // tm,),
            in_specs=[pl.BlockSpec((tm, TOP_K), lambda j, p1, p2: (j, 0)),
                      pl.BlockSpec((tm, d), lambda j, p1, p2: (j, 0)),
                      pl.BlockSpec((1, 1, d), lambda j, p1, p2: (j // per_seq, 0, 0)),
                      pl.BlockSpec((1, d), lambda j, p1, p2: (0, 0)),
                      pl.BlockSpec(memory_space=pl.ANY)],
            out_specs=pl.BlockSpec((tm, d), lambda j, p1, p2: (j, 0)),
            scratch_shapes=[pltpu.VMEM((2, TOP_K, tm * r, LANES), F32), pltpu.SemaphoreType.DMA((2,))]),
        out_shape=jax.ShapeDtypeStruct((n, d), F32),
        compiler_params=_cparams(("arbitrary",), 16 * tm * d * 4),
        name="route_combine",
    )(pos1, pos2, tw, x, gt, gpost, y_sorted)


def _moe_call(h, ti, tw, x, gt, gpost, wgu, wdn):
    b, s, d = x.shape
    n = b * s
    n_exp = wgu.shape[0]
    tm = TOKEN_TILE
    ti = ti.transpose(1, 0, 2).reshape(TOP_K, n)
    rank, cnt = _plan_call(ti)
    counts = cnt[:n_exp, 0].astype(I32)
    padded = (counts + tm - 1) // tm * tm
    ends = jnp.cumsum(padded)
    starts = ends - padded
    experts = jnp.arange(n_exp, dtype=I32)
    pos = rank + jnp.sum(jnp.where(ti[..., None] == experts, starts, 0), axis=-1)
    n_tiles = n * TOP_K // tm + n_exp
    n_rows = n_tiles * tm
    row0 = jnp.arange(n_tiles, dtype=I32) * tm
    tile_expert = jnp.minimum(jnp.sum(row0[:, None] >= ends[None, :], axis=1), n_exp - 1).astype(I32)
    tile_valid = (row0 < ends[-1]).astype(I32)
    pad_lo = jnp.concatenate([starts + counts, ends[-1:]])
    pad_hi = jnp.concatenate([ends, jnp.full((1,), n_rows, I32)])
    pad_ranges = jnp.stack([pad_lo, pad_hi], axis=1).reshape(-1).astype(I32)
    pos1, pos2 = pos[0], pos[1]
    h_sorted = _dispatch_call(pos1, pos2, pad_ranges, h.reshape(-1, LANES), n, n_rows)
    y_sorted = _experts_call(tile_expert, tile_valid, h_sorted, wgu, wdn, tm)
    tw = tw.transpose(0, 2, 1).reshape(n, TOP_K)
    out = _combine_call(pos1, pos2, tw, x.reshape(n, d), gt, gpost, y_sorted, s)
    return out.reshape(b, s, d)


def kernel(x, c, ctx, c_ctx, w_ada, b_ada, norm_g, w_in, rpb, conv_w, out_norm_g, w_out,
           w_gu_dense, w_down_dense, w_router, w_gu_moe, w_down_moe):
    depth = w_ada.shape[0]
    b, s, d = x.shape
    cw = d // 2

    cc =jnp.zeros((16, d), F32).at[:b].set(c).at[b].set(c_ctx)
    mods = _ada_call(cc, w_ada, b_ada)

    for l in range(depth):
        last = l == depth - 1
        m = mods[l].reshape(16, 6, d)
        lat = [m[:b, j][:, None, :] for j in range(6)]
        cxm = [m[b:b + 1, j][:, None, :] for j in range(6)]
        g = [norm_g[l, j][None, :] for j in range(4)]
        w_in_l = w_in[l].astype(BF16)
        w_out_l = w_out[l].astype(BF16)
        gout = out_norm_g[l][None, :]

        q, k, v, bg, vcu = _inproj_call(x, g[0], lat[1], lat[0], w_in_l, True, f"inproj_x{l}")
        if last:
            kc, vc = _inproj_call(ctx, g[0], cxm[1], cxm[0], w_in_l[:, cw:3 * cw], False, f"inproj_c{l}")
        else:
            qc, kc, vc, bgc, vcuc = _inproj_call(ctx, g[0], cxm[1], cxm[0], w_in_l, True, f"inproj_c{l}")

        attn = _na_call(q, k, v, kc, vc, *_bias_tables(rpb[l]))
        moe = l % 2 == 1
        x, h, *route = _merge_call(attn, vcu, bg, x, w_out_l, conv_w[l], gout, g[1], lat[2], g[2],
                                   lat[4], lat[3], w_router[l // 2] if moe else None, f"merge_x{l}")
        if not last:
            attn_c = _ctx_attn_call(qc, kc, vc)
            ctx, hc = _merge_call(attn_c, vcuc, bgc, ctx, w_out_l, conv_w[l], gout, g[1], cxm[2], g[2],
                                  cxm[4], cxm[3], None, f"merge_c{l}")

        if moe:
            x = _moe_call(h, route[0], route[1], x, lat[5], g[3],
                          w_gu_moe[l // 2].astype(BF16), w_down_moe[l // 2].astype(BF16))
        else:
            wgu = w_gu_dense[l // 2].astype(BF16)
            wdn = w_down_dense[l // 2].astype(BF16)
            x = _ffn_call(h, x, wgu, wdn, g[3], lat[5], f"ffn_x{l}")
            if not last:
                ctx = _ffn_call(hc, ctx, wgu, wdn, g[3], cxm[5], f"ffn_c{l}")
    return x
```
